```python
import math
import jax
import jax.numpy as jnp
from jax import lax
import numpy as np

D_MODEL = 1024
BATCH = 16
SEQ = 2048
DEPTH = 2

GRID_W = 64
CTX_LEN = 256
EXPAND = 2
D_INNER = EXPAND * D_MODEL
A_HEADS = 8
A_HEAD_DIM = 64
A_V_DIM = 2 * A_HEAD_DIM
A_WIDTH = A_HEADS * A_V_DIM
POOL_WIDTH = D_INNER - A_WIDTH
POOL_WINDOWS = (2, 4, 8, 16)
POOL_GROUP = POOL_WIDTH // len(POOL_WINDOWS)
A_SPLITS = (A_WIDTH, 2 * A_WIDTH, 3 * A_WIDTH, 3 * A_WIDTH + POOL_WIDTH)
A_IN_COLS = 3 * A_WIDTH + POOL_WIDTH + D_INNER
Q_BLOCK = 128
ROPE_THETA = 10000.0
H_ORDER = 2
H_WIDTH = D_INNER
H_IN_COLS = (H_ORDER + 1) * H_WIDTH + D_INNER
SHORT_CONV = 3
FILTER_EMB = 33
FILTER_BANDS = (FILTER_EMB - 1) // 2
FILTER_HIDDEN = 64
DECAY_TARGET = 1e-2
FAST_DECAY_PCT = 0.3
SLOW_DECAY_PCT = 1.5
RMS_EPS = 1e-6
SUBLN_EPS = 1e-5
N_EVEN = (DEPTH + 1) // 2
N_ODD = DEPTH // 2

kernel_name = 'hybrid_diffattn_pool_hyena_block'


def rmsnorm(x, g, eps=RMS_EPS):
    xf = x.astype(jnp.float32)
    y = xf * lax.rsqrt(jnp.mean(xf * xf, axis=-1, keepdims=True) + eps)
    return (y * g.astype(jnp.float32)).astype(x.dtype)


def _ada(cond, w, b):
    return jnp.split(jax.nn.silu(cond) @ w + b, 3, axis=-1)


def _modulate(x, g, shift, scale):
    return rmsnorm(x, g) * (1 + scale) + shift


def _ctx_read_later(i):
    return any(j % 2 == 0 for j in range(i + 1, DEPTH))


def axial_rope_tables(rows, dtype):
    row = jnp.repeat(jnp.arange(rows), GRID_W).astype(jnp.float32)
    col = jnp.tile(jnp.arange(GRID_W), rows).astype(jnp.float32)
    half = A_HEAD_DIM // 2
    inv = ROPE_THETA ** (-jnp.arange(0, half, 2, dtype=jnp.float32) / half)
    ar, ac = row[:, None] * inv, col[:, None] * inv
    ang = jnp.concatenate([ar, ar, ac, ac], axis=-1)
    return jnp.cos(ang).astype(dtype), jnp.sin(ang).astype(dtype)


def apply_rope(t, cos, sin):
    r1, r2, c1, c2 = jnp.split(t, 4, axis=-1)
    rot = jnp.concatenate([-r2, r1, -c2, c1], axis=-1)
    return t * cos[:, None, :] + rot * sin[:, None, :]


def _split_heads_qk(t, cos=None, sin=None):
    b, n = t.shape[:2]
    t = t.reshape(b, n, 2 * A_HEADS, A_HEAD_DIM)
    if cos is not None:
        t = apply_rope(t, cos, sin)
    return t.reshape(b, n, A_HEADS, 2, A_HEAD_DIM)


def diff_attention(q, k, v, lam):
    s = jnp.einsum('bqhid,bkhid->bhiqk', q, k).astype(jnp.float32) * (A_HEAD_DIM ** -0.5)
    p = jax.nn.softmax(s, axis=-1)
    a = p[:, :, 0] - lam * p[:, :, 1]
    return jnp.einsum('bhqk,bkhe->bqhe', a.astype(v.dtype), v)


def blocked_diff_attention(q, k, v, lam):
    b, n = q.shape[:2]
    nb = n // Q_BLOCK
    qb = q.reshape(b, nb, Q_BLOCK, *q.shape[2:]).swapaxes(0, 1)
    ob = lax.map(lambda qi: diff_attention(qi, k, v, lam), qb)
    return ob.swapaxes(0, 1).reshape(b, n, *ob.shape[3:])


def multiscale_pool(p, pool_w, pool_scale):
    n = p.shape[1]
    pf = p.astype(jnp.float32)
    cs = jnp.concatenate([jnp.zeros_like(pf[:, :1]), jnp.cumsum(pf, axis=1)], axis=1)
    t = jnp.arange(n)
    outs = []
    for gi, win in enumerate(POOL_WINDOWS):
        lo = jnp.clip(t - win // 2, 0, n)
        hi = jnp.clip(t + win - win // 2, 0, n)
        sl = slice(gi * POOL_GROUP, (gi + 1) * POOL_GROUP)
        csg = cs[..., sl]
        mean = (csg[:, hi] - csg[:, lo]) / (hi - lo).astype(jnp.float32)[:, None]
        outs.append(mean - pf[..., sl])
    m = jnp.stack(outs, axis=2).astype(p.dtype)
    y = jnp.einsum('blgc,gcd->blgd', m, pool_w).reshape(p.shape)
    return y * pool_scale


def _diffpool_out(o, p, g, subln_g, pool_w, pool_scale, w_out, lam_init):
    o = rmsnorm(o, subln_g, SUBLN_EPS) * (1.0 - lam_init)
    o = o.reshape(*o.shape[:2], A_WIDTH)
    y = jnp.concatenate([o, multiscale_pool(p, pool_w, pool_scale)], axis=-1)
    return (y * jax.nn.silu(g)) @ w_out


def diffpool_layer(h_lat, h_ctx, cos, sin, w_in, lq1, lk1, lq2, lk2, subln_g, pool_w, pool_scale,
                   w_out, lam_init, update_ctx):
    f32 = jnp.float32
    lam = (jnp.exp(jnp.sum(lq1.astype(f32) * lk1.astype(f32)))
           - jnp.exp(jnp.sum(lq2.astype(f32) * lk2.astype(f32))) + lam_init)
    q, k, v, p, g = jnp.split(h_lat @ w_in, A_SPLITS, axis=-1)
    q = _split_heads_qk(q, cos, sin)
    k = _split_heads_qk(k, cos, sin)
    v = v.reshape(*v.shape[:2], A_HEADS, A_V_DIM)
    if update_ctx:
        q_c, k_c, v_c, p_c, g_c = jnp.split(h_ctx @ w_in, A_SPLITS, axis=-1)
    else:
        k_c, v_c = jnp.split(h_ctx @ w_in[:, A_WIDTH:3 * A_WIDTH], 2, axis=-1)
    k_c = _split_heads_qk(k_c)
    v_c = v_c.reshape(*v_c.shape[:2], A_HEADS, A_V_DIM)
    k_all = jnp.concatenate([k_c, k], axis=1)
    v_all = jnp.concatenate([v_c, v], axis=1)
    o = blocked_diff_attention(q, k_all, v_all, lam)
    y_lat = _diffpool_out(o, p, g, subln_g, pool_w, pool_scale, w_out, lam_init)
    y_ctx = None
    if update_ctx:
        o_c = diff_attention(_split_heads_qk(q_c), k_c, v_c, lam)
        y_ctx = _diffpool_out(o_c, p_c, g_c, subln_g, pool_w, pool_scale, w_out, lam_init)
    return y_lat, y_ctx


def short_conv(u, w, b):
    n = u.shape[1]
    pad = SHORT_CONV // 2
    up = jnp.pad(u, ((0, 0), (pad, SHORT_CONV - 1 - pad), (0, 0)))
    y = b
    for j in range(SHORT_CONV):
        y = y + up[:, j:j + n] * w[j]
    return y


def hyena_filters(n, w0, b0, f0, w1, b1, f1, w2, b2, f2, wout):
    f32 = jnp.float32
    t = jnp.linspace(0.0, 1.0, n, dtype=f32)[:, None]
    w = 2.0 * math.pi * jnp.arange(n, dtype=f32)[:, None] / n
    bands = jnp.linspace(1e-4, FILTER_BANDS - 1, FILTER_BANDS, dtype=f32)[None, :]
    z = jnp.concatenate([t, jnp.cos(bands * w), -jnp.sin(bands * w)], axis=-1)
    h = jnp.sin(f0.astype(f32) * (z @ w0.astype(f32) + b0.astype(f32)))
    h = jnp.sin(f1.astype(f32) * (h @ w1.astype(f32) + b1.astype(f32)))
    h = jnp.sin(f2.astype(f32) * (h @ w2.astype(f32) + b2.astype(f32)))
    h = (h @ wout.astype(f32)).reshape(n, H_ORDER, 2, H_WIDTH)
    max_decay = math.log(DECAY_TARGET) / FAST_DECAY_PCT
    min_decay = math.log(DECAY_TARGET) / SLOW_DECAY_PCT
    deltas = jnp.linspace(min_decay, max_decay, H_WIDTH, dtype=f32)
    decay = jnp.exp(-t * jnp.abs(deltas))
    h = h * decay[:, None, None, :]
    fwd, bwd = h[:, :, 0], h[:, :, 1]
    g = jnp.concatenate([fwd, jnp.zeros_like(fwd[:1]), bwd[1:][::-1]], axis=0)
    return jnp.fft.rfft(g, axis=0)


def long_conv(u, gf, bias):
    n = u.shape[1]
    uf32 = u.astype(jnp.float32)
    uf = jnp.fft.rfft(uf32, n=2 * n, axis=1)
    y = jnp.fft.irfft(uf * gf[None], n=2 * n, axis=1)[:, :n]
    return (y + uf32 * bias.astype(jnp.float32)).astype(u.dtype)


def hyena_mixer(h, w_in, conv_w, conv_b, w0, b0, f0, w1, b1, f1, w2, b2, f2, wout, fbias, w_out):
    proj = h @ w_in
    u = short_conv(proj[..., :(H_ORDER + 1) * H_WIDTH], conv_w, conv_b)
    gate = proj[..., (H_ORDER + 1) * H_WIDTH:]
    v, x1, x2 = jnp.split(u, H_ORDER + 1, axis=-1)
    gf = hyena_filters(h.shape[1], w0, b0, f0, w1, b1, f1, w2, b2, f2, wout)
    z = x1 * long_conv(v, gf[:, 0], fbias[0])
    z = x2 * long_conv(z, gf[:, 1], fbias[1])
    return (z * jax.nn.silu(gate)) @ w_out


def setup_inputs(seed: int = 0) -> dict:
    key = jax.random.key(seed)
    ks = iter(jax.random.split(key, 40))
    nrm = lambda shape, s: jax.random.normal(next(ks), shape, jnp.float32) * s
    D, E, FH = D_MODEL, D_INNER, FILTER_HIDDEN
    return {
        'x': nrm((BATCH, SEQ, D), 1.0),
        'c': nrm((BATCH, D), 1.0),
        'ctx': nrm((BATCH, CTX_LEN, D), 1.0),
        'c_ctx': nrm((D,), 1.0),
        'norm_g': 1.0 + nrm((DEPTH, D), 0.02),
        'ada_w': nrm((DEPTH, D, 3 * D), 0.2 * D ** -0.5),
        'ada_b': nrm((DEPTH, 3 * D), 0.02),
        'final_g': 1.0 + nrm((D,), 0.02),
        'a_w_in': nrm((N_EVEN, D, A_IN_COLS), D ** -0.5),
        'a_lam_q1': nrm((N_EVEN, A_HEAD_DIM), 0.1),
        'a_lam_k1': nrm((N_EVEN, A_HEAD_DIM), 0.1),
        'a_lam_q2': nrm((N_EVEN, A_HEAD_DIM), 0.1),
        'a_lam_k2': nrm((N_EVEN, A_HEAD_DIM), 0.1),
        'a_subln_g': 1.0 + nrm((N_EVEN, A_V_DIM), 0.02),
        'a_pool_w': nrm((N_EVEN, len(POOL_WINDOWS), POOL_GROUP, POOL_GROUP), POOL_GROUP ** -0.5),
        'a_pool_scale': 0.5 + nrm((N_EVEN, POOL_WIDTH), 0.05),
        'a_w_out': nrm((N_EVEN, E, D), E ** -0.5),
        'h_w_in': nrm((N_ODD, D, H_IN_COLS), D ** -0.5),
        'h_conv_w': nrm((N_ODD, SHORT_CONV, (H_ORDER + 1) * H_WIDTH), SHORT_CONV ** -0.5),
        'h_conv_b': nrm((N_ODD, (H_ORDER + 1) * H_WIDTH), 0.02),
        'h_filt_w0': nrm((N_ODD, FILTER_EMB, FH), FILTER_EMB ** -0.5),
        'h_filt_b0': nrm((N_ODD, FH), 0.1),
        'h_filt_f0': 1.0 + nrm((N_ODD, FH), 0.1),
        'h_filt_w1': nrm((N_ODD, FH, FH), FH ** -0.5),
        'h_filt_b1': nrm((N_ODD, FH), 0.1),
        'h_filt_f1': 1.0 + nrm((N_ODD, FH), 0.1),
        'h_filt_w2': nrm((N_ODD, FH, FH), FH ** -0.5),
        'h_filt_b2': nrm((N_ODD, FH), 0.1),
        'h_filt_f2': 1.0 + nrm((N_ODD, FH), 0.1),
        'h_filt_wout': nrm((N_ODD, FH, 2 * H_ORDER * H_WIDTH), 0.02),
        'h_filt_bias': nrm((N_ODD, H_ORDER, H_WIDTH), 0.5),
        'h_w_out': nrm((N_ODD, E, D), E ** -0.5),
    }


def reference(x, c, ctx, c_ctx, norm_g, ada_w, ada_b, final_g, a_w_in, a_lam_q1, a_lam_k1, a_lam_q2,
              a_lam_k2, a_subln_g, a_pool_w, a_pool_scale, a_w_out, h_w_in, h_conv_w, h_conv_b,
              h_filt_w0, h_filt_b0, h_filt_f0, h_filt_w1, h_filt_b1, h_filt_f1, h_filt_w2, h_filt_b2,
              h_filt_f2, h_filt_wout, h_filt_bias, h_w_out):
    n = x.shape[1]
    rows = n // GRID_W
    cos, sin = axial_rope_tables(rows, x.dtype)
    ctx_stream = ctx
    for i in range(DEPTH):
        even = (i % 2 == 0)
        update_ctx = _ctx_read_later(i)
        shift, scale, gate = _ada(c[:, None, :], ada_w[i], ada_b[i])
        h_lat = _modulate(x, norm_g[i], shift, scale)
        h_ctx = None
        if even or update_ctx:
            shift_c, scale_c, gate_c = _ada(c_ctx[None, :], ada_w[i], ada_b[i])
            h_ctx = _modulate(ctx_stream, norm_g[i], shift_c, scale_c)
        if even:
            e = i // 2
            y_lat, y_ctx = diffpool_layer(
                h_lat, h_ctx, cos, sin, a_w_in[e], a_lam_q1[e], a_lam_k1[e], a_lam_q2[e], a_lam_k2[e],
                a_subln_g[e], a_pool_w[e], a_pool_scale[e], a_w_out[e],
                0.8 - 0.6 * math.exp(-0.3 * i), update_ctx)
        else:
            o = i // 2
            hp = (h_w_in[o], h_conv_w[o], h_conv_b[o], h_filt_w0[o], h_filt_b0[o], h_filt_f0[o],
                  h_filt_w1[o], h_filt_b1[o], h_filt_f1[o], h_filt_w2[o], h_filt_b2[o], h_filt_f2[o],
                  h_filt_wout[o], h_filt_bias[o], h_w_out[o])
            y_lat = hyena_mixer(h_lat, *hp)
            y_ctx = hyena_mixer(h_ctx, *hp) if update_ctx else None
        x = x + gate * y_lat
        if update_ctx:
            ctx_stream = ctx_stream + gate_c * y_ctx
    return rmsnorm(x, final_g)
```

```python
import functools
import math

import jax
import jax.numpy as jnp
from jax import lax
from jax.experimental import pallas as pl
from jax.experimental.pallas import tpu as pltpu

F32 = jnp.float32
BF16 = jnp.bfloat16
HIGHEST = lax.Precision.HIGHEST

GRID_W = 64
A_HEADS = 8
A_HEAD_DIM = 64
A_V_DIM = 2 * A_HEAD_DIM
POOL_WINDOWS = (2, 4, 8, 16)
ROPE_THETA = 10000.0
H_ORDER = 2
SHORT_CONV = 3
FILTER_EMB = 33
FILTER_BANDS = (FILTER_EMB - 1) // 2
DECAY_TARGET = 1e-2
FAST_DECAY_PCT = 0.3
SLOW_DECAY_PCT = 1.5
RMS_EPS = 1e-6
SUBLN_EPS = 1e-5
LOG2E = 1.4426950408889634

V7X_LANES = 128
V7X_SUBLANES = 8
V7X_MXU_DIM = 256
V7X_VMEM_BYTES = 64 * 1024 * 1024
VMEM_LIMIT_BYTES = V7X_VMEM_BYTES - 6 * 1024 * 1024

ROW_TILE = 512
Q_TILE = 512
CH_TILE = V7X_MXU_DIM
POOL_HALO = V7X_SUBLANES
CONV_ROWS = 256
CONV_HALO = V7X_SUBLANES
COND_ROWS = 24


def _params(n_axes):
    return pltpu.CompilerParams(dimension_semantics=("arbitrary",) * n_axes,
                                vmem_limit_bytes=VMEM_LIMIT_BYTES)


def _silu(v):
    return v * jax.nn.sigmoid(v)


def _modulate(x, g, shift, scale):
    y = x * lax.rsqrt(jnp.mean(x * x, axis=-1, keepdims=True) + RMS_EPS)
    return (y * g) * (1.0 + scale) + shift


def _nt_dot(a, b):
    return lax.dot_general(a, b, (((1,), (1,)), ((), ())), preferred_element_type=F32)


def _ada_kernel(cond_ref, w_ref, b_ref, o_ref):
    s = _silu(cond_ref[...])
    o_ref[0] = jnp.dot(s, w_ref[0], preferred_element_type=F32, precision=HIGHEST) + b_ref[0]


def _ada(cond, ada_w, ada_b):
    depth, d, d3 = ada_w.shape
    nt = d3 // d
    return pl.pallas_call(
        _ada_kernel,
        out_shape=jax.ShapeDtypeStruct((depth, COND_ROWS, d3), F32),
        grid=(depth, nt),
        in_specs=[pl.BlockSpec((COND_ROWS, d), lambda i, j: (0, 0)),
                  pl.BlockSpec((1, d, d), lambda i, j: (i, 0, j)),
                  pl.BlockSpec((1, 1, d), lambda i, j: (i, 0, j))],
        out_specs=pl.BlockSpec((1, COND_ROWS, d), lambda i, j: (i, 0, j)),
        compiler_params=_params(2), name="ada",
    )(cond, ada_w, ada_b.reshape(depth, 1, d3))


def _rope_slab(t, cos, sin_lo, sin_hi):
    return (t * cos + pltpu.roll(t, V7X_LANES - 16, axis=1) * sin_lo
            + pltpu.roll(t, 16, axis=1) * sin_hi)


def _proj0_kernel(x_ref, ng_ref, sh_ref, sc_ref, cos_ref, slo_ref, shi_ref, w_ref,
                  q_ref, k_ref, v_ref, p_ref, g_ref, *, d, q_scale):
    hb = _modulate(x_ref[0], ng_ref[...], sh_ref[0], sc_ref[0]).astype(BF16)
    cos, slo, shi = cos_ref[...], slo_ref[...], shi_ref[...]

    def proj(c0, width):
        return jnp.dot(hb, w_ref[:, c0:c0 + width], preferred_element_type=F32)

    def rope(t, scale):
        for s in range(d // V7X_LANES):
            sl = slice(s * V7X_LANES, (s + 1) * V7X_LANES)
            yield sl, _rope_slab(t[:, sl], cos, slo, shi) * scale

    for sl, r in rope(proj(0, d), q_scale):
        q_ref[0, :, sl] = r.astype(BF16)
    for sl, r in rope(proj(d, d), 1.0):
        k_ref[0, :, sl] = r.astype(BF16)
    v_ref[0] = proj(2 * d, d).astype(BF16)
    p_ref[0] = proj(3 * d, d)
    g_ref[0] = proj(4 * d, 2 * d)


def _proj0(x, norm_g, shift, scale, cos, sin_lo, sin_hi, w_bf16):
    b, n, d = x.shape
    ncols = w_bf16.shape[1]
    tl = ROW_TILE
    row = lambda j, i: (i, j, 0)
    vec = lambda j, i: (i, 0, 0)
    tab = lambda j, i: (j, 0)
    kern = functools.partial(_proj0_kernel, d=d, q_scale=A_HEAD_DIM ** -0.5 * LOG2E)
    return pl.pallas_call(
        kern,
        out_shape=(jax.ShapeDtypeStruct((b, n, d), BF16), jax.ShapeDtypeStruct((b, n, d), BF16),
                   jax.ShapeDtypeStruct((b, n, d), BF16), jax.ShapeDtypeStruct((b, n, d), F32),
                   jax.ShapeDtypeStruct((b, n, 2 * d), F32)),
        grid=(n // tl, b),
        in_specs=[pl.BlockSpec((1, tl, d), row),
                  pl.BlockSpec((1, d), lambda j, i: (0, 0)),
                  pl.BlockSpec((1, 1, d), vec), pl.BlockSpec((1, 1, d), vec),
                  pl.BlockSpec((tl, V7X_LANES), tab), pl.BlockSpec((tl, V7X_LANES), tab),
                  pl.BlockSpec((tl, V7X_LANES), tab),
                  pl.BlockSpec((d, ncols), lambda j, i: (0, 0), pipeline_mode=pl.Buffered(1))],
        out_specs=(pl.BlockSpec((1, tl, d), row), pl.BlockSpec((1, tl, d), row),
                   pl.BlockSpec((1, tl, d), row), pl.BlockSpec((1, tl, d), row),
                   pl.BlockSpec((1, tl, 2 * d), row)),
        compiler_params=_params(2), name="proj0",
    )(x, norm_g, shift, scale, cos, sin_lo, sin_hi, w_bf16)


def _ctx_kv_kernel(c_ref, ng_ref, sh_ref, sc_ref, wk_ref, wv_ref, k_ref, v_ref):
    hb = _modulate(c_ref[...], ng_ref[...], sh_ref[...], sc_ref[...]).astype(BF16)
    k_ref[...] = jnp.dot(hb, wk_ref[...], preferred_element_type=F32).astype(BF16)
    v_ref[...] = jnp.dot(hb, wv_ref[...], preferred_element_type=F32).astype(BF16)


def _ctx_kv(ctx_rows, norm_g, shift_c, scale_c, w_bf16):
    rows, d = ctx_rows.shape
    tr = ROW_TILE
    one = lambda i: (0, 0)
    return pl.pallas_call(
        _ctx_kv_kernel,
        out_shape=(jax.ShapeDtypeStruct((rows, d), BF16), jax.ShapeDtypeStruct((rows, d), BF16)),
        grid=(rows // tr,),
        in_specs=[pl.BlockSpec((tr, d), lambda i: (i, 0)),
                  pl.BlockSpec((1, d), one), pl.BlockSpec((1, d), one), pl.BlockSpec((1, d), one),
                  pl.BlockSpec((d, d), lambda i: (0, 1)), pl.BlockSpec((d, d), lambda i: (0, 2))],
        out_specs=(pl.BlockSpec((tr, d), lambda i: (i, 0)), pl.BlockSpec((tr, d), lambda i: (i, 0))),
        compiler_params=_params(1), name="ctx_kv",
    )(ctx_rows, norm_g, shift_c, scale_c, w_bf16, w_bf16)


def _attn_kernel(q_ref, kc_ref, k_ref, vc_ref, v_ref, lq1_ref, lk1_ref, lq2_ref, lk2_ref, sg_ref,
                 o_ref, *, lam_init):
    q = q_ref[0]
    kc, k, vc, v = kc_ref[0], k_ref[0], vc_ref[0], v_ref[0]
    lane = lax.broadcasted_iota(jnp.int32, q.shape, 1)

    def component(mask):
        qm = jnp.where(mask, q, jnp.zeros_like(q))
        sc = _nt_dot(qm, kc)
        sl = _nt_dot(qm, k)
        m = jnp.maximum(jnp.max(sc, axis=-1, keepdims=True), jnp.max(sl, axis=-1, keepdims=True))
        pc = jnp.exp2(sc - m)
        pq = jnp.exp2(sl - m)
        denom = jnp.sum(pc, axis=-1, keepdims=True) + jnp.sum(pq, axis=-1, keepdims=True)
        o = (jnp.dot(pc.astype(BF16), vc, preferred_element_type=F32)
             + jnp.dot(pq.astype(BF16), v, preferred_element_type=F32))
        return o / denom

    o1 = component(lane < A_HEAD_DIM)
    o2 = component(lane >= A_HEAD_DIM)
    lam = (jnp.exp(jnp.sum(lq1_ref[...] * lk1_ref[...], axis=-1, keepdims=True))
           - jnp.exp(jnp.sum(lq2_ref[...] * lk2_ref[...], axis=-1, keepdims=True)) + lam_init)
    o = o1 - lam * o2
    on = o * lax.rsqrt(jnp.mean(o * o, axis=-1, keepdims=True) + SUBLN_EPS)
    o_ref[0] = (on * sg_ref[...]) * (1.0 - lam_init)


def _attention(q, kc, k, vc, v, lq1, lk1, lq2, lk2, subln_g, lam_init):
    b, n, d = q.shape
    nc = kc.shape[1]
    hd = A_V_DIM
    tq = Q_TILE
    kv = lambda bi, h, i: (bi, 0, h)
    one = lambda bi, h, i: (0, 0)
    kern = functools.partial(_attn_kernel, lam_init=lam_init)
    return pl.pallas_call(
        kern,
        out_shape=jax.ShapeDtypeStruct((b, n, d), F32),
        grid=(b, A_HEADS, n // tq),
        in_specs=[pl.BlockSpec((1, tq, hd), lambda bi, h, i: (bi, i, h)),
                  pl.BlockSpec((1, nc, hd), kv), pl.BlockSpec((1, n, hd), kv),
                  pl.BlockSpec((1, nc, hd), kv), pl.BlockSpec((1, n, hd), kv),
                  pl.BlockSpec((1, A_HEAD_DIM), one), pl.BlockSpec((1, A_HEAD_DIM), one),
                  pl.BlockSpec((1, A_HEAD_DIM), one), pl.BlockSpec((1, A_HEAD_DIM), one),
                  pl.BlockSpec((1, hd), one)],
        out_specs=pl.BlockSpec((1, tq, hd), lambda bi, h, i: (bi, i, h)),
        compiler_params=_params(3), name="diff_attn",
    )(q, kc, k, vc, v, lq1, lk1, lq2, lk2, subln_g)


def _pool_kernel(p_ref, pw_ref, ps_ref, o_ref, pad_ref, *, n, group):
    rows = ROW_TILE
    zeros = jnp.zeros((POOL_HALO, group), F32)
    for gi, win in enumerate(POOL_WINDOWS):
        sl = slice(gi * group, (gi + 1) * group)
        pad_ref[0:POOL_HALO, :] = zeros
        pad_ref[POOL_HALO + n:2 * POOL_HALO + n, :] = zeros
        pad_ref[POOL_HALO:POOL_HALO + n, :] = p_ref[0, :, sl]
        back = win // 2
        for r0 in range(0, n, rows):
            acc = pad_ref[pl.ds(POOL_HALO + r0 - back, rows), :]
            for j in range(1 - back, win - back):
                acc = acc + pad_ref[pl.ds(POOL_HALO + r0 + j, rows), :]
            t = r0 + lax.broadcasted_iota(jnp.int32, (rows, 1), 0)
            cnt = (jnp.minimum(t + (win - back), n) - jnp.maximum(t - back, 0)).astype(F32)
            m = acc / cnt - pad_ref[pl.ds(POOL_HALO + r0, rows), :]
            y = jnp.dot(m.astype(BF16), pw_ref[gi], preferred_element_type=F32)
            o_ref[0, r0:r0 + rows, sl] = y * ps_ref[:, sl]


def _pool(p, pool_w_bf16, pool_scale):
    b, n, width = p.shape
    ng, group, _ = pool_w_bf16.shape
    kern = functools.partial(_pool_kernel, n=n, group=group)
    return pl.pallas_call(
        kern,
        out_shape=jax.ShapeDtypeStruct((b, n, width), F32),
        grid=(b,),
        in_specs=[pl.BlockSpec((1, n, width), lambda i: (i, 0, 0)),
                  pl.BlockSpec((ng, group, group), lambda i: (0, 0, 0)),
                  pl.BlockSpec((1, width), lambda i: (0, 0))],
        out_specs=pl.BlockSpec((1, n, width), lambda i: (i, 0, 0)),
        scratch_shapes=[pltpu.VMEM((n + 2 * POOL_HALO, group), F32)],
        compiler_params=_params(1), name="pool",
    )(p, pool_w_bf16, pool_scale)


def _out0_kernel(o_ref, yp_ref, g_ref, x_ref, gate_ref, w_ref, ng_ref, sh_ref, sc_ref,
                 x1_ref, h1_ref, *, aw):
    sg = _silu(g_ref[0])
    a = (o_ref[0] * sg[:, :aw]).astype(BF16)
    c = (yp_ref[0] * sg[:, aw:]).astype(BF16)
    y = (jnp.dot(a, w_ref[0:aw, :], preferred_element_type=F32)
         + jnp.dot(c, w_ref[aw:, :], preferred_element_type=F32))
    x1 = x_ref[0] + gate_ref[0] * y
    x1_ref[0] = x1
    h1_ref[0] = _modulate(x1, ng_ref[...], sh_ref[0], sc_ref[0]).astype(BF16)


def _out0(o, ypool, g, x, gate, w_bf16, norm_g1, shift1, scale1):
    b, n, d = x.shape
    aw = o.shape[2]
    e = g.shape[2]
    tl = ROW_TILE
    row = lambda i, j: (i, j, 0)
    vec = lambda i, j: (i, 0, 0)
    kern = functools.partial(_out0_kernel, aw=aw)
    return pl.pallas_call(
        kern,
        out_shape=(jax.ShapeDtypeStruct((b, n, d), F32), jax.ShapeDtypeStruct((b, n, d), BF16)),
        grid=(b, n // tl),
        in_specs=[pl.BlockSpec((1, tl, aw), row), pl.BlockSpec((1, tl, e - aw), row),
                  pl.BlockSpec((1, tl, e), row), pl.BlockSpec((1, tl, d), row),
                  pl.BlockSpec((1, 1, d), vec),
                  pl.BlockSpec((e, d), lambda i, j: (0, 0)),
                  pl.BlockSpec((1, d), lambda i, j: (0, 0)),
                  pl.BlockSpec((1, 1, d), vec), pl.BlockSpec((1, 1, d), vec)],
        out_specs=(pl.BlockSpec((1, tl, d), row), pl.BlockSpec((1, tl, d), row)),
        compiler_params=_params(2), name="out0",
    )(o, ypool, g, x, gate, w_bf16, norm_g1, shift1, scale1)


def _filter_mlp_kernel(z_ref, w0, b0, f0, w1, b1, f1, w2, b2, f2, o_ref):
    def layer(h, w, b, f):
        return jnp.sin(f[...] * (jnp.dot(h, w[...], preferred_element_type=F32, precision=HIGHEST) + b[...]))
    h = layer(z_ref[...], w0, b0, f0)
    h = layer(h, w1, b1, f1)
    o_ref[...] = layer(h, w2, b2, f2)


def _filter_mlp(z, w0, b0, f0, w1, b1, f1, w2, b2, f2):
    return pl.pallas_call(
        _filter_mlp_kernel,
        out_shape=jax.ShapeDtypeStruct(z.shape, F32),
        compiler_params=pltpu.CompilerParams(vmem_limit_bytes=VMEM_LIMIT_BYTES), name="filter_mlp",
    )(z, w0, b0, f0, w1, b1, f1, w2, b2, f2)


def _row_chunks(n, body, init=None):
    def step(c, carry):
        return body(pl.multiple_of(c * CONV_ROWS, CONV_ROWS), carry)
    return lax.fori_loop(0, n // CONV_ROWS, step, init)


def _parity_sign():
    row = lax.broadcasted_iota(jnp.int32, (CONV_ROWS, CH_TILE), 0)
    return jnp.where(row % 2 == 0, 1.0, -1.0)


def _filter_spec_kernel(h_ref, wf_ref, wb_ref, t_ref, d_ref, cs_ref,
                        gr_ref, gi_ref, gn_ref, even_ref, odd_ref, *, n):
    inv_n = 1.0 / (2 * n)
    sgn = _parity_sign()
    row = lax.broadcasted_iota(jnp.int32, (CONV_ROWS, CH_TILE), 0)

    def taps(r0, nyq):
        rows = pl.ds(r0, CONV_ROWS)
        hm = h_ref[rows, :]
        decay = jnp.exp(-t_ref[rows, :] * d_ref[...])
        fwd = jnp.dot(hm, wf_ref[...], preferred_element_type=F32, precision=HIGHEST) * decay
        bwd = jnp.dot(hm, wb_ref[...], preferred_element_type=F32, precision=HIGHEST) * decay
        bwd = jnp.where(row + r0 == 0, 0.0, bwd)
        even = fwd + bwd
        even_ref[rows, :] = even.astype(BF16)
        odd_ref[rows, :] = (bwd - fwd).astype(BF16)
        return nyq + jnp.sum(even * sgn, axis=0, keepdims=True)

    nyq = _row_chunks(n, taps, jnp.zeros((1, CH_TILE), F32))
    gn_ref[0] = nyq * inv_n

    def spectrum(f0, carry):
        rows = pl.ds(f0, CONV_ROWS)
        wgt = jnp.where(row + f0 == 0, inv_n, 2.0 * inv_n)
        gr_ref[0, rows, :] = jnp.dot(cs_ref[rows, 0:n], even_ref[...], preferred_element_type=F32) * wgt
        gi_ref[0, rows, :] = jnp.dot(cs_ref[rows, n:2 * n], odd_ref[...], preferred_element_type=F32) * wgt
        return carry

    _row_chunks(n, spectrum)


def _filter_spec(hmlp, wout, tcol, dabs, csmat):
    n, kpad = hmlp.shape
    e = dabs.shape[1]
    ct = CH_TILE
    nct = e // ct
    const = lambda o, j: (0, 0)
    kern = functools.partial(_filter_spec_kernel, n=n)
    return pl.pallas_call(
        kern,
        out_shape=(jax.ShapeDtypeStruct((H_ORDER, n, e), F32), jax.ShapeDtypeStruct((H_ORDER, n, e), F32),
                   jax.ShapeDtypeStruct((H_ORDER, 1, e), F32)),
        grid=(H_ORDER, nct),
        in_specs=[pl.BlockSpec((n, kpad), const),
                  pl.BlockSpec((kpad, ct), lambda o, j: (0, (2 * o) * nct + j)),
                  pl.BlockSpec((kpad, ct), lambda o, j: (0, (2 * o + 1) * nct + j)),
                  pl.BlockSpec((n, 1), const),
                  pl.BlockSpec((1, ct), lambda o, j: (0, j)),
                  pl.BlockSpec((n, 2 * n), const, pipeline_mode=pl.Buffered(1))],
        out_specs=(pl.BlockSpec((1, n, ct), lambda o, j: (o, 0, j)),
                   pl.BlockSpec((1, n, ct), lambda o, j: (o, 0, j)),
                   pl.BlockSpec((1, 1, ct), lambda o, j: (o, 0, j))),
        scratch_shapes=[pltpu.VMEM((n, ct), BF16), pltpu.VMEM((n, ct), BF16)],
        compiler_params=_params(2), name="filter_spec",
    )(hmlp, wout, wout, tcol, dabs, csmat)


def _hyena_kernel(h_ref, wv_ref, w1_ref, w2_ref, wg_ref, cwv_ref, cw1_ref, cw2_ref,
                  cbv_ref, cb1_ref, cb2_ref, cs_ref,
                  gr0_ref, gi0_ref, gr1_ref, gi1_ref, gn0_ref, gn1_ref, fb0_ref, fb1_ref,
                  o_ref, raw_ref, u_ref, ub_ref, spec_ref, *, n):
    sgn = _parity_sign()
    halo = CONV_HALO
    zero_halo = jnp.zeros((halo, CH_TILE), F32)
    raw_ref[0:halo, :] = zero_halo
    raw_ref[halo + n:2 * halo + n, :] = zero_halo

    def project(w_ref):
        def body(r0, carry):
            raw_ref[pl.ds(halo + r0, CONV_ROWS), :] = jnp.dot(
                h_ref[0, pl.ds(r0, CONV_ROWS), :], w_ref[...], preferred_element_type=F32)
            return carry
        _row_chunks(n, body)

    def short_conv(r0, cw_ref, cb_ref):
        blk = raw_ref[pl.ds(r0, CONV_ROWS + 2 * halo), :]
        prev = blk[halo - 1:halo - 1 + CONV_ROWS]
        cur = blk[halo:halo + CONV_ROWS]
        nxt = blk[halo + 1:halo + 1 + CONV_ROWS]
        return cb_ref[...] + prev * cw_ref[0:1, :] + cur * cw_ref[1:2, :] + nxt * cw_ref[2:3, :]

    def set_input(r0, u, nyq):
        rows = pl.ds(r0, CONV_ROWS)
        u_ref[rows, :] = u
        ub_ref[rows, :] = u.astype(BF16)
        return nyq + jnp.sum(u * sgn, axis=0, keepdims=True)

    def forward(gr_ref, gi_ref):
        def body(f0, carry):
            rows = pl.ds(f0, CONV_ROWS)
            ur = jnp.dot(cs_ref[rows, 0:n], ub_ref[...], preferred_element_type=F32)
            us = jnp.dot(cs_ref[rows, n:2 * n], ub_ref[...], preferred_element_type=F32)
            gr, gi = gr_ref[0, rows, :], gi_ref[0, rows, :]
            spec_ref[rows, :] = (ur * gr + us * gi).astype(BF16)
            spec_ref[pl.ds(n + f0, CONV_ROWS), :] = (us * gr - ur * gi).astype(BF16)
            return carry
        _row_chunks(n, body)

    def inverse(t0, nyq, gn_ref, fb_ref):
        rows = pl.ds(t0, CONV_ROWS)
        y = jnp.dot(cs_ref[rows, :], spec_ref[...], preferred_element_type=F32)
        return y + sgn * (nyq * gn_ref[0]) + u_ref[rows, :] * fb_ref[0]

    zero = jnp.zeros((1, CH_TILE), F32)

    project(wv_ref)
    nyq0 = _row_chunks(n, lambda r0, nyq: set_input(r0, short_conv(r0, cwv_ref, cbv_ref), nyq), zero)
    forward(gr0_ref, gi0_ref)
    project(w1_ref)
    nyq1 = _row_chunks(
        n, lambda r0, nyq: set_input(
            r0, short_conv(r0, cw1_ref, cb1_ref) * inverse(r0, nyq0, gn0_ref, fb0_ref), nyq), zero)
    forward(gr1_ref, gi1_ref)
    project(w2_ref)

    def finish(r0, carry):
        rows = pl.ds(r0, CONV_ROWS)
        z = short_conv(r0, cw2_ref, cb2_ref) * inverse(r0, nyq1, gn1_ref, fb1_ref)
        gate = jnp.dot(h_ref[0, rows, :], wg_ref[...], preferred_element_type=F32)
        o_ref[0, rows, :] = (z * _silu(gate)).astype(BF16)
        return carry
    _row_chunks(n, finish)


def _hyena(h1, w_bf16, conv_w, conv_b, csmat, gr, gi, gn, fbias):
    b, n, d = h1.shape
    e = gr.shape[2]
    ct = CH_TILE
    nct = e // ct
    wspec = lambda k: pl.BlockSpec((d, ct), lambda j, i, k=k: (0, k * nct + j))
    cwspec = lambda k: pl.BlockSpec((SHORT_CONV, ct), lambda j, i, k=k: (0, k * nct + j))
    cbspec = lambda k: pl.BlockSpec((1, ct), lambda j, i, k=k: (0, k * nct + j))
    const = pl.BlockSpec((n, 2 * n), lambda j, i: (0, 0), pipeline_mode=pl.Buffered(1))
    gspec = lambda o: pl.BlockSpec((1, n, ct), lambda j, i, o=o: (o, 0, j), pipeline_mode=pl.Buffered(1))
    vspec = lambda o: pl.BlockSpec((1, 1, ct), lambda j, i, o=o: (o, 0, j))
    kern = functools.partial(_hyena_kernel, n=n)
    return pl.pallas_call(
        kern,
        out_shape=jax.ShapeDtypeStruct((b, n, e), BF16),
        grid=(nct, b),
        in_specs=[pl.BlockSpec((1, n, d), lambda j, i: (i, 0, 0)),
                  wspec(0), wspec(1), wspec(2), wspec(3),
                  cwspec(0), cwspec(1), cwspec(2), cbspec(0), cbspec(1), cbspec(2),
                  const,
                  gspec(0), gspec(0), gspec(1), gspec(1), vspec(0), vspec(1), vspec(0), vspec(1)],
        out_specs=pl.BlockSpec((1, n, ct), lambda j, i: (i, 0, j)),
        scratch_shapes=[pltpu.VMEM((n + 2 * CONV_HALO, ct), F32), pltpu.VMEM((n, ct), F32),
                        pltpu.VMEM((n, ct), BF16), pltpu.VMEM((2 * n, ct), BF16)],
        compiler_params=_params(2), name="hyena_mix",
    )(h1, w_bf16, w_bf16, w_bf16, w_bf16, conv_w, conv_w, conv_w, conv_b, conv_b, conv_b,
      csmat, gr, gi, gr, gi, gn, gn, fbias, fbias)


def _out1_kernel(z_ref, x_ref, gate_ref, w_ref, fg_ref, o_ref):
    y = jnp.dot(z_ref[0], w_ref[...], preferred_element_type=F32)
    x2 = x_ref[0] + gate_ref[0] * y
    o_ref[0] = (x2 * lax.rsqrt(jnp.mean(x2 * x2, axis=-1, keepdims=True) + RMS_EPS)) * fg_ref[...]


def _out1(zg, x1, gate, w_bf16, final_g):
    b, n, d = x1.shape
    e = zg.shape[2]
    tl = ROW_TILE
    row = lambda i, j: (i, j, 0)
    return pl.pallas_call(
        _out1_kernel,
        out_shape=jax.ShapeDtypeStruct((b, n, d), F32),
        grid=(b, n // tl),
        in_specs=[pl.BlockSpec((1, tl, e), row), pl.BlockSpec((1, tl, d), row),
                  pl.BlockSpec((1, 1, d), lambda i, j: (i, 0, 0)),
                  pl.BlockSpec((e, d), lambda i, j: (0, 0)),
                  pl.BlockSpec((1, d), lambda i, j: (0, 0))],
        out_specs=pl.BlockSpec((1, tl, d), row),
        compiler_params=_params(2), name="out1",
    )(zg, x1, gate, w_bf16, final_g)


def _rope_tables(n):
    rows = n // GRID_W
    row = jnp.repeat(jnp.arange(rows), GRID_W).astype(F32)
    col = jnp.tile(jnp.arange(GRID_W), rows).astype(F32)
    half = A_HEAD_DIM // 2
    inv = ROPE_THETA ** (-jnp.arange(0, half, 2, dtype=F32) / half)
    ar, ac = row[:, None] * inv, col[:, None] * inv
    ang = jnp.concatenate([ar, ar, ac, ac], axis=-1)
    cos, sin = jnp.cos(ang), jnp.sin(ang)
    first = (jnp.arange(A_HEAD_DIM) % (half)) < (half // 2)
    sin_lo = jnp.where(first, -sin, 0.0)
    sin_hi = jnp.where(first, 0.0, sin)
    rep = V7X_LANES // A_HEAD_DIM
    return jnp.tile(cos, (1, rep)), jnp.tile(sin_lo, (1, rep)), jnp.tile(sin_hi, (1, rep))


def _dft_tables(n):
    idx = jnp.arange(n, dtype=jnp.int32)
    phase = (idx[:, None] * idx[None, :]) % (2 * n)
    ang = phase.astype(F32) * (math.pi / n)
    return jnp.concatenate([jnp.cos(ang), jnp.sin(ang)], axis=1).astype(BF16)


def _filter_features(n, kpad):
    t = jnp.linspace(0.0, 1.0, n, dtype=F32)[:, None]
    w = 2.0 * math.pi * jnp.arange(n, dtype=F32)[:, None] / n
    bands = jnp.linspace(1e-4, FILTER_BANDS - 1, FILTER_BANDS, dtype=F32)[None, :]
    z = jnp.concatenate([t, jnp.cos(bands * w), -jnp.sin(bands * w)], axis=-1)
    return jnp.pad(z, ((0, 0), (0, kpad - z.shape[1]))), t


def _pad2(a, rows, cols):
    return jnp.pad(a, ((0, rows - a.shape[0]), (0, cols - a.shape[1])))


def kernel(x, c, ctx, c_ctx, norm_g, ada_w, ada_b, final_g, a_w_in, a_lam_q1, a_lam_k1, a_lam_q2, a_lam_k2, a_subln_g, a_pool_w, a_pool_scale, a_w_out, h_w_in, h_conv_w, h_conv_b, h_filt_w0, h_filt_b0, h_filt_f0, h_filt_w1, h_filt_b1, h_filt_f1, h_filt_w2, h_filt_b2, h_filt_f2, h_filt_wout, h_filt_bias, h_w_out):
    b, n, d = x.shape
    nc = ctx.shape[1]
    e = h_w_out.shape[1]
    assert norm_g.shape[0] == 2 and a_w_in.shape[0] == 1 and h_w_in.shape[0] == 1
    assert b + 1 <= COND_ROWS and n % ROW_TILE == 0 and n % Q_TILE == 0 and (b * nc) % ROW_TILE == 0
    assert d == A_HEADS * A_V_DIM and e % CH_TILE == 0

    cond = jnp.concatenate([c, c_ctx[None, :], jnp.zeros((COND_ROWS - b - 1, d), F32)], axis=0)
    mods = _ada(cond, ada_w, ada_b)
    shift = mods[:, :, 0:d]
    scale = mods[:, :, d:2 * d]
    gate = mods[:, :, 2 * d:3 * d]
    per_batch = lambda m, i: m[i, :b].reshape(b, 1, d)

    lam_init = 0.8 - 0.6 * math.exp(-0.3 * 0)
    w0 = a_w_in[0].astype(BF16)
    cos, sin_lo, sin_hi = _rope_tables(n)
    q, k, v, p, g = _proj0(x, norm_g[0:1], per_batch(shift, 0), per_batch(scale, 0), cos, sin_lo, sin_hi, w0)
    kc, vc = _ctx_kv(ctx.reshape(b * nc, d), norm_g[0:1], shift[0, b:b + 1], scale[0, b:b + 1], w0)
    o = _attention(q, kc.reshape(b, nc, d), k, vc.reshape(b, nc, d), v,
                   a_lam_q1, a_lam_k1, a_lam_q2, a_lam_k2, a_subln_g, lam_init)
    ypool = _pool(p, a_pool_w[0].astype(BF16), a_pool_scale)
    x1, h1 = _out0(o, ypool, g, x, per_batch(gate, 0), a_w_out[0].astype(BF16),
                   norm_g[1:2], per_batch(shift, 1), per_batch(scale, 1))

    kpad = V7X_LANES
    z, tcol = _filter_features(n, kpad)
    row1 = lambda a: _pad2(a, 1, kpad)
    hmlp = _filter_mlp(z, _pad2(h_filt_w0[0], kpad, kpad), row1(h_filt_b0), row1(h_filt_f0),
                       _pad2(h_filt_w1[0], kpad, kpad), row1(h_filt_b1), row1(h_filt_f1),
                       _pad2(h_filt_w2[0], kpad, kpad), row1(h_filt_b2), row1(h_filt_f2))
    max_decay = math.log(DECAY_TARGET) / FAST_DECAY_PCT
    min_decay = math.log(DECAY_TARGET) / SLOW_DECAY_PCT
    dabs = jnp.abs(jnp.linspace(min_decay, max_decay, e, dtype=F32))[None, :]
    csmat = _dft_tables(n)
    wout = _pad2(h_filt_wout[0], kpad, h_filt_wout.shape[2])
    gr, gi, gn = _filter_spec(hmlp, wout, tcol, dabs, csmat)
    zg = _hyena(h1, h_w_in[0].astype(BF16), h_conv_w[0], h_conv_b, csmat, gr, gi, gn,
                h_filt_bias[0].reshape(H_ORDER, 1, e))
    return _out1(zg, x1, per_batch(gate, 1), h_w_out[0].astype(BF16), final_g[None, :])
```

```python
import functools
import math

import jax
import jax.numpy as jnp
from jax import lax
from jax.experimental import pallas as pl
from jax.experimental.pallas import tpu as pltpu

F32 = jnp.float32
BF16 = jnp.bfloat16
HIGHEST = lax.Precision.HIGHEST

GRID_W = 64
A_HEADS = 8
A_HEAD_DIM = 64
A_V_DIM = 2 * A_HEAD_DIM
POOL_WINDOWS = (2, 4, 8, 16)
ROPE_THETA = 10000.0
H_ORDER = 2
SHORT_CONV = 3
FILTER_EMB = 33
FILTER_BANDS = (FILTER_EMB - 1) // 2
DECAY_TARGET = 1e-2
FAST_DECAY_PCT = 0.3
SLOW_DECAY_PCT = 1.5
RMS_EPS = 1e-6
SUBLN_EPS = 1e-5
LOG2E = 1.4426950408889634

V7X_LANES = 128
V7X_SUBLANES = 8
V7X_MXU_DIM = 256
V7X_VMEM_BYTES = 64 * 1024 * 1024
VMEM_LIMIT_BYTES = V7X_VMEM_BYTES - 6 * 1024 * 1024

ROW_TILE = 512
Q_TILE = 512
Q_SUB = Q_TILE
CH_TILE = V7X_MXU_DIM
POOL_HALO = V7X_SUBLANES
CONV_ROWS = 256
CONV_HALO = V7X_SUBLANES
FFT_BLOCK = V7X_MXU_DIM
FFT_RADIX = 16
FFT_RES = FFT_RADIX // 2 + 1
FFT_ROWS = 16
SQRT_HALF = math.sqrt(0.5)
COND_ROWS = 24


def _params(n_axes):
    return pltpu.CompilerParams(dimension_semantics=("arbitrary",) * n_axes,
                                vmem_limit_bytes=VMEM_LIMIT_BYTES)


def _silu(v):
    return v * jax.nn.sigmoid(v)


def _modulate(x, g, shift, scale):
    y = x * lax.rsqrt(jnp.mean(x * x, axis=-1, keepdims=True) + RMS_EPS)
    return (y * g) * (1.0 + scale) + shift


def _nt_dot(a, b):
    return lax.dot_general(a, b, (((1,), (1,)), ((), ())), preferred_element_type=F32)


def _ada_kernel(cond_ref, w_ref, b_ref, o_ref):
    s = _silu(cond_ref[...])
    o_ref[0] = jnp.dot(s, w_ref[0], preferred_element_type=F32, precision=HIGHEST) + b_ref[0]


def _ada(cond, ada_w, ada_b):
    depth, d, d3 = ada_w.shape
    nt = d3 // d
    return pl.pallas_call(
        _ada_kernel,
        out_shape=jax.ShapeDtypeStruct((depth, COND_ROWS, d3), F32),
        grid=(depth, nt),
        in_specs=[pl.BlockSpec((COND_ROWS, d), lambda i, j: (0, 0)),
                  pl.BlockSpec((1, d, d), lambda i, j: (i, 0, j)),
                  pl.BlockSpec((1, 1, d), lambda i, j: (i, 0, j))],
        out_specs=pl.BlockSpec((1, COND_ROWS, d), lambda i, j: (i, 0, j)),
        compiler_params=_params(2), name="ada",
    )(cond, ada_w, ada_b.reshape(depth, 1, d3))


def _rope_slab(t, cos, sin_lo, sin_hi):
    return (t * cos + pltpu.roll(t, V7X_LANES - 16, axis=1) * sin_lo
            + pltpu.roll(t, 16, axis=1) * sin_hi)


def _proj0_kernel(x_ref, ng_ref, sh_ref, sc_ref, cos_ref, slo_ref, shi_ref, w_ref, wvt_ref,
                  q_ref, k_ref, vt_ref, p_ref, g_ref, *, d, q_scale):
    hb = _modulate(x_ref[0], ng_ref[...], sh_ref[0], sc_ref[0]).astype(BF16)
    cos, slo, shi = cos_ref[...], slo_ref[...], shi_ref[...]

    def proj(c0, width):
        return jnp.dot(hb, w_ref[:, c0:c0 + width], preferred_element_type=F32)

    def rope(t, scale):
        for s in range(d // V7X_LANES):
            sl = slice(s * V7X_LANES, (s + 1) * V7X_LANES)
            yield sl, _rope_slab(t[:, sl], cos, slo, shi) * scale

    for sl, r in rope(proj(0, d), q_scale):
        q_ref[0, :, sl] = r.astype(BF16)
    for sl, r in rope(proj(d, d), 1.0):
        k_ref[0, :, sl] = r.astype(BF16)
    vt_ref[0] = _nt_dot(wvt_ref[...], hb).astype(BF16)
    p_ref[0] = proj(3 * d, d)
    g_ref[0] = proj(4 * d, 2 * d)


def _proj0(x, norm_g, shift, scale, cos, sin_lo, sin_hi, w_bf16, wvt_bf16):
    b, n, d = x.shape
    ncols = w_bf16.shape[1]
    tl = ROW_TILE
    row = lambda j, i: (i, j, 0)
    vec = lambda j, i: (i, 0, 0)
    tab = lambda j, i: (j, 0)
    kern = functools.partial(_proj0_kernel, d=d, q_scale=A_HEAD_DIM ** -0.5 * LOG2E)
    return pl.pallas_call(
        kern,
        out_shape=(jax.ShapeDtypeStruct((b, n, d), BF16), jax.ShapeDtypeStruct((b, n, d), BF16),
                   jax.ShapeDtypeStruct((b, d, n), BF16), jax.ShapeDtypeStruct((b, n, d), F32),
                   jax.ShapeDtypeStruct((b, n, 2 * d), F32)),
        grid=(n // tl, b),
        in_specs=[pl.BlockSpec((1, tl, d), row),
                  pl.BlockSpec((1, d), lambda j, i: (0, 0)),
                  pl.BlockSpec((1, 1, d), vec), pl.BlockSpec((1, 1, d), vec),
                  pl.BlockSpec((tl, V7X_LANES), tab), pl.BlockSpec((tl, V7X_LANES), tab),
                  pl.BlockSpec((tl, V7X_LANES), tab),
                  pl.BlockSpec((d, ncols), lambda j, i: (0, 0), pipeline_mode=pl.Buffered(1)),
                  pl.BlockSpec((d, d), lambda j, i: (0, 0), pipeline_mode=pl.Buffered(1))],
        out_specs=(pl.BlockSpec((1, tl, d), row), pl.BlockSpec((1, tl, d), row),
                   pl.BlockSpec((1, d, tl), lambda j, i: (i, 0, j)), pl.BlockSpec((1, tl, d), row),
                   pl.BlockSpec((1, tl, 2 * d), row)),
        compiler_params=_params(2), name="proj0",
    )(x, norm_g, shift, scale, cos, sin_lo, sin_hi, w_bf16, wvt_bf16)


def _ctx_kv_kernel(c_ref, ng_ref, sh_ref, sc_ref, wk_ref, wvt_ref, k_ref, vt_ref):
    hb = _modulate(c_ref[...], ng_ref[...], sh_ref[...], sc_ref[...]).astype(BF16)
    k_ref[...] = jnp.dot(hb, wk_ref[...], preferred_element_type=F32).astype(BF16)
    vt_ref[...] = _nt_dot(wvt_ref[...], hb).astype(BF16)


def _ctx_kv(ctx_rows, norm_g, shift_c, scale_c, w_bf16, wvt_bf16):
    rows, d = ctx_rows.shape
    tr = ROW_TILE
    one = lambda i: (0, 0)
    return pl.pallas_call(
        _ctx_kv_kernel,
        out_shape=(jax.ShapeDtypeStruct((rows, d), BF16), jax.ShapeDtypeStruct((d, rows), BF16)),
        grid=(rows // tr,),
        in_specs=[pl.BlockSpec((tr, d), lambda i: (i, 0)),
                  pl.BlockSpec((1, d), one), pl.BlockSpec((1, d), one), pl.BlockSpec((1, d), one),
                  pl.BlockSpec((d, d), lambda i: (0, 1)), pl.BlockSpec((d, d), one)],
        out_specs=(pl.BlockSpec((tr, d), lambda i: (i, 0)), pl.BlockSpec((d, tr), lambda i: (0, i))),
        compiler_params=_params(1), name="ctx_kv",
    )(ctx_rows, norm_g, shift_c, scale_c, w_bf16, wvt_bf16)


def _attn_kernel(q_ref, kc_ref, k_ref, vct_ref, vt_ref, lq1_ref, lk1_ref, lq2_ref, lk2_ref, sg_ref,
                 o_ref, *, lam_init):
    lam = (jnp.exp(jnp.sum(lq1_ref[...] * lk1_ref[...], axis=-1, keepdims=True))
           - jnp.exp(jnp.sum(lq2_ref[...] * lk2_ref[...], axis=-1, keepdims=True)) + lam_init)
    lane = lax.broadcasted_iota(jnp.int32, (Q_SUB, A_V_DIM), 1)

    for s in range(q_ref.shape[1] // Q_SUB):
        rows = slice(s * Q_SUB, (s + 1) * Q_SUB)
        q = q_ref[0, rows, :]

        def component(mask):
            qm = jnp.where(mask, q, jnp.zeros_like(q))
            sc = _nt_dot(kc_ref[0], qm)
            sl = _nt_dot(k_ref[0], qm)
            m = jnp.maximum(jnp.max(sc, axis=0, keepdims=True), jnp.max(sl, axis=0, keepdims=True))
            pc = jnp.exp2(sc - m)
            pq = jnp.exp2(sl - m)
            denom = jnp.sum(pc, axis=0, keepdims=True) + jnp.sum(pq, axis=0, keepdims=True)
            return pc, pq, denom

        pc1, pq1, l1 = component(lane < A_HEAD_DIM)
        pc2, pq2, l2 = component(lane >= A_HEAD_DIM)
        r1 = 1.0 / l1
        r2 = lam / l2
        ac = (pc1 * r1 - pc2 * r2).astype(BF16)
        aq = (pq1 * r1 - pq2 * r2).astype(BF16)
        ot = (jnp.dot(vct_ref[...], ac, preferred_element_type=F32)
              + jnp.dot(vt_ref[0], aq, preferred_element_type=F32))
        on = ot * lax.rsqrt(jnp.mean(ot * ot, axis=0, keepdims=True) + SUBLN_EPS)
        o_ref[0, rows, :] = (on.T * sg_ref[...]) * (1.0 - lam_init)


def _attention(q, kc, k, vct, vt, lq1, lk1, lq2, lk2, subln_g, lam_init):
    b, n, d = q.shape
    nc = kc.shape[1]
    hd = A_V_DIM
    tq = Q_TILE
    kv = lambda bi, h, i: (bi, 0, h)
    one = lambda bi, h, i: (0, 0)
    kern = functools.partial(_attn_kernel, lam_init=lam_init)
    return pl.pallas_call(
        kern,
        out_shape=jax.ShapeDtypeStruct((b, n, d), F32),
        grid=(b, A_HEADS, n // tq),
        in_specs=[pl.BlockSpec((1, tq, hd), lambda bi, h, i: (bi, i, h)),
                  pl.BlockSpec((1, nc, hd), kv), pl.BlockSpec((1, n, hd), kv),
                  pl.BlockSpec((hd, nc), lambda bi, h, i: (h, bi)),
                  pl.BlockSpec((1, hd, n), lambda bi, h, i: (bi, h, 0)),
                  pl.BlockSpec((1, A_HEAD_DIM), one), pl.BlockSpec((1, A_HEAD_DIM), one),
                  pl.BlockSpec((1, A_HEAD_DIM), one), pl.BlockSpec((1, A_HEAD_DIM), one),
                  pl.BlockSpec((1, hd), one)],
        out_specs=pl.BlockSpec((1, tq, hd), lambda bi, h, i: (bi, i, h)),
        compiler_params=_params(3), name="diff_attn",
    )(q, kc, k, vct, vt, lq1, lk1, lq2, lk2, subln_g)


def _pool_kernel(p_ref, pw_ref, ps_ref, o_ref, pad_ref, *, n, group):
    rows = ROW_TILE
    zeros = jnp.zeros((POOL_HALO, group), F32)
    for gi, win in enumerate(POOL_WINDOWS):
        sl = slice(gi * group, (gi + 1) * group)
        pad_ref[0:POOL_HALO, :] = zeros
        pad_ref[POOL_HALO + n:2 * POOL_HALO + n, :] = zeros
        pad_ref[POOL_HALO:POOL_HALO + n, :] = p_ref[0, :, sl]
        back = win // 2
        for r0 in range(0, n, rows):
            acc = pad_ref[pl.ds(POOL_HALO + r0 - back, rows), :]
            for j in range(1 - back, win - back):
                acc = acc + pad_ref[pl.ds(POOL_HALO + r0 + j, rows), :]
            t = r0 + lax.broadcasted_iota(jnp.int32, (rows, 1), 0)
            cnt = (jnp.minimum(t + (win - back), n) - jnp.maximum(t - back, 0)).astype(F32)
            m = acc / cnt - pad_ref[pl.ds(POOL_HALO + r0, rows), :]
            y = jnp.dot(m.astype(BF16), pw_ref[gi], preferred_element_type=F32)
            o_ref[0, r0:r0 + rows, sl] = y * ps_ref[:, sl]


def _pool(p, pool_w_bf16, pool_scale):
    b, n, width = p.shape
    ng, group, _ = pool_w_bf16.shape
    kern = functools.partial(_pool_kernel, n=n, group=group)
    return pl.pallas_call(
        kern,
        out_shape=jax.ShapeDtypeStruct((b, n, width), F32),
        grid=(b,),
        in_specs=[pl.BlockSpec((1, n, width), lambda i: (i, 0, 0)),
                  pl.BlockSpec((ng, group, group), lambda i: (0, 0, 0)),
                  pl.BlockSpec((1, width), lambda i: (0, 0))],
        out_specs=pl.BlockSpec((1, n, width), lambda i: (i, 0, 0)),
        scratch_shapes=[pltpu.VMEM((n + 2 * POOL_HALO, group), F32)],
        compiler_params=_params(1), name="pool",
    )(p, pool_w_bf16, pool_scale)


def _out0_kernel(o_ref, yp_ref, g_ref, x_ref, gate_ref, w_ref, ng_ref, sh_ref, sc_ref,
                 x1_ref, h1_ref, *, aw):
    sg = _silu(g_ref[0])
    a = (o_ref[0] * sg[:, :aw]).astype(BF16)
    c = (yp_ref[0] * sg[:, aw:]).astype(BF16)
    y = (jnp.dot(a, w_ref[0:aw, :], preferred_element_type=F32)
         + jnp.dot(c, w_ref[aw:, :], preferred_element_type=F32))
    x1 = x_ref[0] + gate_ref[0] * y
    x1_ref[0] = x1
    h1_ref[0] = _modulate(x1, ng_ref[...], sh_ref[0], sc_ref[0]).astype(BF16)


def _out0(o, ypool, g, x, gate, w_bf16, norm_g1, shift1, scale1):
    b, n, d = x.shape
    aw = o.shape[2]
    e = g.shape[2]
    tl = ROW_TILE
    row = lambda i, j: (i, j, 0)
    vec = lambda i, j: (i, 0, 0)
    kern = functools.partial(_out0_kernel, aw=aw)
    return pl.pallas_call(
        kern,
        out_shape=(jax.ShapeDtypeStruct((b, n, d), F32), jax.ShapeDtypeStruct((b, n, d), BF16)),
        grid=(b, n // tl),
        in_specs=[pl.BlockSpec((1, tl, aw), row), pl.BlockSpec((1, tl, e - aw), row),
                  pl.BlockSpec((1, tl, e), row), pl.BlockSpec((1, tl, d), row),
                  pl.BlockSpec((1, 1, d), vec),
                  pl.BlockSpec((e, d), lambda i, j: (0, 0)),
                  pl.BlockSpec((1, d), lambda i, j: (0, 0)),
                  pl.BlockSpec((1, 1, d), vec), pl.BlockSpec((1, 1, d), vec)],
        out_specs=(pl.BlockSpec((1, tl, d), row), pl.BlockSpec((1, tl, d), row)),
        compiler_params=_params(2), name="out0",
    )(o, ypool, g, x, gate, w_bf16, norm_g1, shift1, scale1)


def _filter_mlp_kernel(z_ref, w0, b0, f0, w1, b1, f1, w2, b2, f2, o_ref):
    def layer(h, w, b, f):
        return jnp.sin(f[...] * (jnp.dot(h, w[...], preferred_element_type=F32, precision=HIGHEST) + b[...]))
    h = layer(z_ref[...], w0, b0, f0)
    h = layer(h, w1, b1, f1)
    o_ref[...] = layer(h, w2, b2, f2)


def _filter_mlp(z, w0, b0, f0, w1, b1, f1, w2, b2, f2):
    return pl.pallas_call(
        _filter_mlp_kernel,
        out_shape=jax.ShapeDtypeStruct(z.shape, F32),
        compiler_params=pltpu.CompilerParams(vmem_limit_bytes=VMEM_LIMIT_BYTES), name="filter_mlp",
    )(z, w0, b0, f0, w1, b1, f1, w2, b2, f2)


def _row_chunks(n, body, init=None):
    def step(c, carry):
        return body(pl.multiple_of(c * CONV_ROWS, CONV_ROWS), carry)
    return lax.fori_loop(0, n // CONV_ROWS, step, init)


def _dft8_half(x0, x2, x4, x6):
    s04, d04, s26, d26 = x0 + x4, x0 - x4, x2 + x6, x2 - x6
    cd, cs = SQRT_HALF * d26, SQRT_HALF * s26
    re = (s04 + s26, x0 + cd, d04, x0 - cd, s04 - s26)
    im = (None, -(x4 + cs), -d26, x4 - cs, None)
    return re, im


def _fft16_forward(x):
    er, ei = _dft8_half(x[0], x[2], x[4], x[6])
    orr, oi = _dft8_half(x[1], x[3], x[5], x[7])
    tr, ti = [None] * FFT_RES, [None] * FFT_RES
    tr[0] = er[0] + orr[0]
    tr[8] = er[0] - orr[0]
    tr[4], ti[4] = er[4], -orr[4]
    for k in (1, 2, 3):
        wr, wi = math.cos(math.pi * k / 8), -math.sin(math.pi * k / 8)
        pr = wr * orr[k] - wi * oi[k]
        pi = wr * oi[k] + wi * orr[k]
        tr[k], ti[k] = er[k] + pr, ei[k] + pi
        tr[8 - k], ti[8 - k] = er[k] - pr, pi - ei[k]
    return tr, ti


def _fft16_inverse(hr, hi):
    def quad(a0, a4, a1, a2, a3):
        (a1r, a1i), (a2r, a2i), (a3r, a3i) = a1, a2, a3
        lo, hi_ = a0 + a4, a0 - a4
        return (lo + 2.0 * (a1r + a2r + a3r),
                hi_ + 2.0 * (SQRT_HALF * ((a1r - a1i) - (a3r + a3i)) - a2i),
                lo + 2.0 * (a3i - a1i - a2r),
                hi_ + 2.0 * (SQRT_HALF * ((a3r - a3i) - (a1r + a1i)) + a2i))
    even = quad(hr[0] + hr[8], 2.0 * hr[4], *[(hr[k] + hr[8 - k], hi[k] - hi[8 - k]) for k in (1, 2, 3)])
    odd_in = []
    for k in (1, 2, 3):
        dr, di = hr[k] - hr[8 - k], hi[k] + hi[8 - k]
        wr, wi = math.cos(math.pi * k / 8), math.sin(math.pi * k / 8)
        odd_in.append((dr * wr - di * wi, dr * wi + di * wr))
    odd = quad(hr[0] - hr[8], -2.0 * hi[4], *odd_in)
    return [even[0], odd[0], even[1], odd[1], even[2], odd[2], even[3], odd[3]]


def _fft_stage1(u_ref, tw_ref, twc_ref, tws_ref):
    nb = FFT_BLOCK

    def body(i, carry):
        a0 = pl.multiple_of(i * FFT_ROWS, FFT_ROWS)
        x = [u_ref[pl.ds(pl.multiple_of(b * nb + a0, FFT_ROWS), FFT_ROWS), :] for b in range(FFT_RADIX // 2)]
        tr, ti = _fft16_forward(x)
        re_rows = pl.ds(a0, FFT_ROWS)
        im_rows = pl.ds(pl.multiple_of(nb + a0, FFT_ROWS), FFT_ROWS)
        tw_ref[0, re_rows, :] = tr[0].astype(BF16)
        for k in range(1, FFT_RES):
            c, s = twc_ref[k - 1, re_rows, :], tws_ref[k - 1, re_rows, :]
            if ti[k] is None:
                re, im = tr[k] * c, -(tr[k] * s)
            else:
                re, im = tr[k] * c + ti[k] * s, ti[k] * c - tr[k] * s
            tw_ref[k, re_rows, :] = re.astype(BF16)
            tw_ref[k, im_rows, :] = im.astype(BF16)
        return carry

    lax.fori_loop(0, nb // FFT_ROWS, body, None)


def _fft_forward_mxu(tw_ref, d_ref, k):
    if k == 0:
        return jnp.dot(d_ref[:, 0:FFT_BLOCK], tw_ref[0, 0:FFT_BLOCK, :], preferred_element_type=F32)
    return jnp.dot(d_ref[...], tw_ref[k], preferred_element_type=F32)


def _fft_stage2_inverse(h_ref, y_ref):
    nb = FFT_BLOCK

    def body(i, carry):
        a0 = pl.multiple_of(i * V7X_SUBLANES, V7X_SUBLANES)
        re_rows = pl.ds(a0, V7X_SUBLANES)
        im_rows = pl.ds(pl.multiple_of(nb + a0, V7X_SUBLANES), V7X_SUBLANES)
        hr = [h_ref[k, re_rows, :] for k in range(FFT_RES)]
        hi = [None] + [h_ref[k, im_rows, :] for k in range(1, FFT_RES - 1)] + [None]
        for b, yb in enumerate(_fft16_inverse(hr, hi)):
            y_ref[pl.ds(pl.multiple_of(b * nb + a0, V7X_SUBLANES), V7X_SUBLANES), :] = yb
        return carry

    lax.fori_loop(0, nb // V7X_SUBLANES, body, None)


def _filter_spec_kernel(h_ref, wf_ref, wb_ref, t_ref, dcy_ref, d_ref, twc_ref, tws_ref,
                        gr_ref, gi_ref, fwd_ref, bwd_ref, twf_ref, twb_ref, *, n):
    inv_n = 1.0 / (2 * n)
    row = lax.broadcasted_iota(jnp.int32, (CONV_ROWS, CH_TILE), 0)

    def taps(r0, carry):
        rows = pl.ds(r0, CONV_ROWS)
        hm = h_ref[rows, :]
        decay = jnp.exp(-t_ref[rows, :] * dcy_ref[...])
        fwd_ref[rows, :] = jnp.dot(hm, wf_ref[...], preferred_element_type=F32, precision=HIGHEST) * decay
        bwd = jnp.dot(hm, wb_ref[...], preferred_element_type=F32, precision=HIGHEST) * decay
        bwd_ref[rows, :] = jnp.where(row + r0 == 0, 0.0, bwd)
        return carry

    _row_chunks(n, taps)
    _fft_stage1(fwd_ref, twf_ref, twc_ref, tws_ref)
    _fft_stage1(bwd_ref, twb_ref, twc_ref, tws_ref)
    nb = FFT_BLOCK
    for k in range(FFT_RES):
        xf = _fft_forward_mxu(twf_ref, d_ref, k)
        xb = _fft_forward_mxu(twb_ref, d_ref, k)
        gr_ref[0, k] = (xf[0:nb] + xb[0:nb]) * inv_n
        gi_ref[0, k] = (xf[nb:2 * nb] - xb[nb:2 * nb]) * inv_n


def _filter_spec(hmlp, wout, tcol, dabs, dmat, twc, tws):
    n, kpad = hmlp.shape
    e = dabs.shape[1]
    ct = CH_TILE
    nct = e // ct
    nb = FFT_BLOCK
    const = lambda o, j: (0, 0)
    const3 = lambda o, j: (0, 0, 0)
    gshape = jax.ShapeDtypeStruct((H_ORDER, FFT_RES, nb, e), F32)
    gspec = pl.BlockSpec((1, FFT_RES, nb, ct), lambda o, j: (o, 0, 0, j))
    kern = functools.partial(_filter_spec_kernel, n=n)
    return pl.pallas_call(
        kern,
        out_shape=(gshape, gshape),
        grid=(H_ORDER, nct),
        in_specs=[pl.BlockSpec((n, kpad), const),
                  pl.BlockSpec((kpad, ct), lambda o, j: (0, (2 * o) * nct + j)),
                  pl.BlockSpec((kpad, ct), lambda o, j: (0, (2 * o + 1) * nct + j)),
                  pl.BlockSpec((n, 1), const),
                  pl.BlockSpec((1, ct), lambda o, j: (0, j)),
                  pl.BlockSpec((2 * nb, 2 * nb), const),
                  pl.BlockSpec((FFT_RES - 1, nb, ct), const3, pipeline_mode=pl.Buffered(1)),
                  pl.BlockSpec((FFT_RES - 1, nb, ct), const3, pipeline_mode=pl.Buffered(1))],
        out_specs=(gspec, gspec),
        scratch_shapes=[pltpu.VMEM((n, ct), F32), pltpu.VMEM((n, ct), F32),
                        pltpu.VMEM((FFT_RES, 2 * nb, ct), BF16), pltpu.VMEM((FFT_RES, 2 * nb, ct), BF16)],
        compiler_params=_params(2), name="filter_spec",
    )(hmlp, wout, wout, tcol, dabs, dmat, twc, tws)


def _hyena_kernel(h_ref, wv_ref, w1_ref, w2_ref, wg_ref, cwv_ref, cw1_ref, cw2_ref,
                  cbv_ref, cb1_ref, cb2_ref, d_ref, di_ref, twc_ref, tws_ref,
                  gr0_ref, gi0_ref, gr1_ref, gi1_ref, fb0_ref, fb1_ref,
                  o_ref, raw_ref, u_ref, tw_ref, res_ref, y_ref, *, n):
    halo = CONV_HALO
    nb = FFT_BLOCK
    zero_halo = jnp.zeros((halo, CH_TILE), F32)
    raw_ref[0:halo, :] = zero_halo
    raw_ref[halo + n:2 * halo + n, :] = zero_halo

    def project(w_ref):
        def body(r0, carry):
            raw_ref[pl.ds(halo + r0, CONV_ROWS), :] = jnp.dot(
                h_ref[0, pl.ds(r0, CONV_ROWS), :], w_ref[...], preferred_element_type=F32)
            return carry
        _row_chunks(n, body)

    def short_conv(r0, cw_ref, cb_ref):
        blk = raw_ref[pl.ds(r0, CONV_ROWS + 2 * halo), :]
        prev = blk[halo - 1:halo - 1 + CONV_ROWS]
        cur = blk[halo:halo + CONV_ROWS]
        nxt = blk[halo + 1:halo + 1 + CONV_ROWS]
        return cb_ref[...] + prev * cw_ref[0:1, :] + cur * cw_ref[1:2, :] + nxt * cw_ref[2:3, :]

    def long_conv(gr_ref, gi_ref):
        _fft_stage1(u_ref, tw_ref, twc_ref, tws_ref)
        for k in range(FFT_RES):
            x = _fft_forward_mxu(tw_ref, d_ref, k)
            xr, xi = x[0:nb], x[nb:2 * nb]
            gr, gi = gr_ref[0, k], gi_ref[0, k]
            y = jnp.concatenate([xr * gr - xi * gi, xr * gi + xi * gr], axis=0).astype(BF16)
            z = jnp.dot(di_ref[...], y, preferred_element_type=F32)
            zr, zi = z[0:nb], z[nb:2 * nb]
            if k == 0:
                res_ref[0, 0:nb, :] = zr
            else:
                c, s = twc_ref[k - 1], tws_ref[k - 1]
                res_ref[k, 0:nb, :] = zr * c - zi * s
                res_ref[k, nb:2 * nb, :] = zr * s + zi * c
        _fft_stage2_inverse(res_ref, y_ref)

    def mixed(r0, cw_ref, cb_ref, fb_ref):
        rows = pl.ds(r0, CONV_ROWS)
        return short_conv(r0, cw_ref, cb_ref) * (y_ref[rows, :] + u_ref[rows, :] * fb_ref[0])

    def store_u(r0, u):
        u_ref[pl.ds(r0, CONV_ROWS), :] = u

    project(wv_ref)
    _row_chunks(n, lambda r0, c: store_u(r0, short_conv(r0, cwv_ref, cbv_ref)))
    long_conv(gr0_ref, gi0_ref)
    project(w1_ref)
    _row_chunks(n, lambda r0, c: store_u(r0, mixed(r0, cw1_ref, cb1_ref, fb0_ref)))
    long_conv(gr1_ref, gi1_ref)
    project(w2_ref)

    def finish(r0, carry):
        rows = pl.ds(r0, CONV_ROWS)
        z = mixed(r0, cw2_ref, cb2_ref, fb1_ref)
        gate = jnp.dot(h_ref[0, rows, :], wg_ref[...], preferred_element_type=F32)
        o_ref[0, rows, :] = (z * _silu(gate)).astype(BF16)
        return carry
    _row_chunks(n, finish)


def _hyena(h1, w_bf16, conv_w, conv_b, dmat, dimat, twc, tws, gr, gi, fbias):
    b, n, d = h1.shape
    e = gr.shape[3]
    ct = CH_TILE
    nct = e // ct
    nb = FFT_BLOCK
    wspec = lambda k: pl.BlockSpec((d, ct), lambda j, i, k=k: (0, k * nct + j))
    cwspec = lambda k: pl.BlockSpec((SHORT_CONV, ct), lambda j, i, k=k: (0, k * nct + j))
    cbspec = lambda k: pl.BlockSpec((1, ct), lambda j, i, k=k: (0, k * nct + j))
    dspec = pl.BlockSpec((2 * nb, 2 * nb), lambda j, i: (0, 0))
    tspec = pl.BlockSpec((FFT_RES - 1, nb, ct), lambda j, i: (0, 0, 0), pipeline_mode=pl.Buffered(1))
    gspec = lambda o: pl.BlockSpec((1, FFT_RES, nb, ct), lambda j, i, o=o: (o, 0, 0, j),
                                   pipeline_mode=pl.Buffered(1))
    vspec = lambda o: pl.BlockSpec((1, 1, ct), lambda j, i, o=o: (o, 0, j))
    kern = functools.partial(_hyena_kernel, n=n)
    return pl.pallas_call(
        kern,
        out_shape=jax.ShapeDtypeStruct((b, n, e), BF16),
        grid=(nct, b),
        in_specs=[pl.BlockSpec((1, n, d), lambda j, i: (i, 0, 0)),
                  wspec(0), wspec(1), wspec(2), wspec(3),
                  cwspec(0), cwspec(1), cwspec(2), cbspec(0), cbspec(1), cbspec(2),
                  dspec, dspec, tspec, tspec,
                  gspec(0), gspec(0), gspec(1), gspec(1), vspec(0), vspec(1)],
        out_specs=pl.BlockSpec((1, n, ct), lambda j, i: (i, 0, j)),
        scratch_shapes=[pltpu.VMEM((n + 2 * CONV_HALO, ct), F32), pltpu.VMEM((n, ct), F32),
                        pltpu.VMEM((FFT_RES, 2 * nb, ct), BF16), pltpu.VMEM((FFT_RES, 2 * nb, ct), F32),
                        pltpu.VMEM((n, ct), F32)],
        compiler_params=_params(2), name="hyena_mix",
    )(h1, w_bf16, w_bf16, w_bf16, w_bf16, conv_w, conv_w, conv_w, conv_b, conv_b, conv_b,
      dmat, dimat, twc, tws, gr, gi, gr, gi, fbias, fbias)


def _out1_kernel(z_ref, x_ref, gate_ref, w_ref, fg_ref, o_ref):
    y = jnp.dot(z_ref[0], w_ref[...], preferred_element_type=F32)
    x2 = x_ref[0] + gate_ref[0] * y
    o_ref[0] = (x2 * lax.rsqrt(jnp.mean(x2 * x2, axis=-1, keepdims=True) + RMS_EPS)) * fg_ref[...]


def _out1(zg, x1, gate, w_bf16, final_g):
    b, n, d = x1.shape
    e = zg.shape[2]
    tl = ROW_TILE
    row = lambda i, j: (i, j, 0)
    return pl.pallas_call(
        _out1_kernel,
        out_shape=jax.ShapeDtypeStruct((b, n, d), F32),
        grid=(b, n // tl),
        in_specs=[pl.BlockSpec((1, tl, e), row), pl.BlockSpec((1, tl, d), row),
                  pl.BlockSpec((1, 1, d), lambda i, j: (i, 0, 0)),
                  pl.BlockSpec((e, d), lambda i, j: (0, 0)),
                  pl.BlockSpec((1, d), lambda i, j: (0, 0))],
        out_specs=pl.BlockSpec((1, tl, d), row),
        compiler_params=_params(2), name="out1",
    )(zg, x1, gate, w_bf16, final_g)


def _rope_tables(n):
    rows = n // GRID_W
    row = jnp.repeat(jnp.arange(rows), GRID_W).astype(F32)
    col = jnp.tile(jnp.arange(GRID_W), rows).astype(F32)
    half = A_HEAD_DIM // 2
    inv = ROPE_THETA ** (-jnp.arange(0, half, 2, dtype=F32) / half)
    ar, ac = row[:, None] * inv, col[:, None] * inv
    ang = jnp.concatenate([ar, ar, ac, ac], axis=-1)
    cos, sin = jnp.cos(ang), jnp.sin(ang)
    first = (jnp.arange(A_HEAD_DIM) % (half)) < (half // 2)
    sin_lo = jnp.where(first, -sin, 0.0)
    sin_hi = jnp.where(first, 0.0, sin)
    rep = V7X_LANES // A_HEAD_DIM
    return jnp.tile(cos, (1, rep)), jnp.tile(sin_lo, (1, rep)), jnp.tile(sin_hi, (1, rep))


def _fft_tables(n, ch):
    nb = FFT_BLOCK
    idx = jnp.arange(nb, dtype=jnp.int32)
    ang = ((idx[:, None] * idx[None, :]) % nb).astype(F32) * (2.0 * math.pi / nb)
    c, s = jnp.cos(ang), jnp.sin(ang)
    dmat = jnp.block([[c, s], [-s, c]]).astype(BF16)
    dimat = jnp.block([[c, -s], [s, c]]).astype(BF16)
    k2 = jnp.arange(1, FFT_RES, dtype=jnp.int32)
    phi = (k2[:, None] * idx[None, :]).astype(F32) * (math.pi / n)
    twc = jnp.broadcast_to(jnp.cos(phi)[:, :, None], (FFT_RES - 1, nb, ch))
    tws = jnp.broadcast_to(jnp.sin(phi)[:, :, None], (FFT_RES - 1, nb, ch))
    return dmat, dimat, twc, tws


def _filter_features(n, kpad):
    t = jnp.linspace(0.0, 1.0, n, dtype=F32)[:, None]
    w = 2.0 * math.pi * jnp.arange(n, dtype=F32)[:, None] / n
    bands = jnp.linspace(1e-4, FILTER_BANDS - 1, FILTER_BANDS, dtype=F32)[None, :]
    z = jnp.concatenate([t, jnp.cos(bands * w), -jnp.sin(bands * w)], axis=-1)
    return jnp.pad(z, ((0, 0), (0, kpad - z.shape[1]))), t


def _pad2(a, rows, cols):
    return jnp.pad(a, ((0, rows - a.shape[0]), (0, cols - a.shape[1])))


def kernel(x, c, ctx, c_ctx, norm_g, ada_w, ada_b, final_g, a_w_in, a_lam_q1, a_lam_k1, a_lam_q2, a_lam_k2, a_subln_g, a_pool_w, a_pool_scale, a_w_out, h_w_in, h_conv_w, h_conv_b, h_filt_w0, h_filt_b0, h_filt_f0, h_filt_w1, h_filt_b1, h_filt_f1, h_filt_w2, h_filt_b2, h_filt_f2, h_filt_wout, h_filt_bias, h_w_out):
    b, n, d = x.shape
    nc = ctx.shape[1]
    e = h_w_out.shape[1]
    assert norm_g.shape[0] == 2 and a_w_in.shape[0] == 1 and h_w_in.shape[0] == 1
    assert b + 1 <= COND_ROWS and n % ROW_TILE == 0 and n % Q_TILE == 0 and (b * nc) % ROW_TILE == 0
    assert d == A_HEADS * A_V_DIM and e % CH_TILE == 0 and 2 * n == FFT_BLOCK * FFT_RADIX

    cond = jnp.concatenate([c, c_ctx[None, :], jnp.zeros((COND_ROWS - b - 1, d), F32)], axis=0)
    mods = _ada(cond, ada_w, ada_b)
    shift = mods[:, :, 0:d]
    scale = mods[:, :, d:2 * d]
    gate = mods[:, :, 2 * d:3 * d]
    per_batch = lambda m, i: m[i, :b].reshape(b, 1, d)

    lam_init = 0.8 - 0.6 * math.exp(-0.3 * 0)
    w0 = a_w_in[0].astype(BF16)
    cos, sin_lo, sin_hi = _rope_tables(n)
    wvt = a_w_in[0, :, 2 * d:3 * d].T.astype(BF16)
    q, k, vt, p, g = _proj0(x, norm_g[0:1], per_batch(shift, 0), per_batch(scale, 0), cos, sin_lo, sin_hi,
                            w0, wvt)
    kc, vct = _ctx_kv(ctx.reshape(b * nc, d), norm_g[0:1], shift[0, b:b + 1], scale[0, b:b + 1], w0, wvt)
    o = _attention(q, kc.reshape(b, nc, d), k, vct, vt,
                   a_lam_q1, a_lam_k1, a_lam_q2, a_lam_k2, a_subln_g, lam_init)
    ypool = _pool(p, a_pool_w[0].astype(BF16), a_pool_scale)
    x1, h1 = _out0(o, ypool, g, x, per_batch(gate, 0), a_w_out[0].astype(BF16),
                   norm_g[1:2], per_batch(shift, 1), per_batch(scale, 1))

    kpad = V7X_LANES
    z, tcol = _filter_features(n, kpad)
    row1 = lambda a: _pad2(a, 1, kpad)
    hmlp = _filter_mlp(z, _pad2(h_filt_w0[0], kpad, kpad), row1(h_filt_b0), row1(h_filt_f0),
                       _pad2(h_filt_w1[0], kpad, kpad), row1(h_filt_b1), row1(h_filt_f1),
                       _pad2(h_filt_w2[0], kpad, kpad), row1(h_filt_b2), row1(h_filt_f2))
    max_decay = math.log(DECAY_TARGET) / FAST_DECAY_PCT
    min_decay = math.log(DECAY_TARGET) / SLOW_DECAY_PCT
    dabs = jnp.abs(jnp.linspace(min_decay, max_decay, e, dtype=F32))[None, :]
    dmat, dimat, twc, tws = _fft_tables(n, CH_TILE)
    wout = _pad2(h_filt_wout[0], kpad, h_filt_wout.shape[2])
    gr, gi = _filter_spec(hmlp, wout, tcol, dabs, dmat, twc, tws)
    zg = _hyena(h1, h_w_in[0].astype(BF16), h_conv_w[0], h_conv_b, dmat, dimat, twc, tws, gr, gi,
                h_filt_bias[0].reshape(H_ORDER, 1, e))
    return _out1(zg, x1, per_batch(gate, 1), h_w_out[0].astype(BF16), final_g[None, :])
```

```python
import functools
import math

import jax
import jax.numpy as jnp
from jax import lax
from jax.experimental import pallas as pl
from jax.experimental.pallas import tpu as pltpu

F32 = jnp.float32
BF16 = jnp.bfloat16
HIGHEST = lax.Precision.HIGHEST

GRID_W = 64
A_HEADS = 8
A_HEAD_DIM = 64
A_V_DIM = 2 * A_HEAD_DIM
POOL_WINDOWS = (2, 4, 8, 16)
ROPE_THETA = 10000.0
H_ORDER = 2
SHORT_CONV = 3
FILTER_EMB = 33
FILTER_BANDS = (FILTER_EMB - 1) // 2
DECAY_TARGET = 1e-2
FAST_DECAY_PCT = 0.3
SLOW_DECAY_PCT = 1.5
RMS_EPS = 1e-6
SUBLN_EPS = 1e-5
LOG2E = 1.4426950408889634

V7X_LANES = 128
V7X_SUBLANES = 8
V7X_MXU_DIM = 256
V7X_VMEM_BYTES = 64 * 1024 * 1024
VMEM_LIMIT_BYTES = V7X_VMEM_BYTES - 6 * 1024 * 1024

ROW_TILE = 512
Q_TILE = 1024
Q_SUB = V7X_MXU_DIM
KEY_CHUNK = V7X_MXU_DIM
CH_TILE = V7X_MXU_DIM
POOL_HALO = V7X_SUBLANES
CONV_ROWS = 256
CONV_HALO = V7X_SUBLANES
FFT_BLOCK = V7X_MXU_DIM
FFT_RADIX = 16
FFT_RES = FFT_RADIX // 2 + 1
FFT_ROWS = 16
SQRT_HALF = math.sqrt(0.5)
COND_ROWS = 24


def _params(n_axes):
    return pltpu.CompilerParams(dimension_semantics=("arbitrary",) * n_axes,
                                vmem_limit_bytes=VMEM_LIMIT_BYTES)


def _silu(v):
    return v * jax.nn.sigmoid(v)


def _modulate(x, g, shift, scale):
    y = x * lax.rsqrt(jnp.mean(x * x, axis=-1, keepdims=True) + RMS_EPS)
    return (y * g) * (1.0 + scale) + shift


def _nt_dot(a, b):
    return lax.dot_general(a, b, (((1,), (1,)), ((), ())), preferred_element_type=F32)


def _ada_kernel(cond_ref, w_ref, b_ref, o_ref):
    s = _silu(cond_ref[...])
    o_ref[0] = jnp.dot(s, w_ref[0], preferred_element_type=F32, precision=HIGHEST) + b_ref[0]


def _ada(cond, ada_w, ada_b):
    depth, d, d3 = ada_w.shape
    nt = d3 // d
    return pl.pallas_call(
        _ada_kernel,
        out_shape=jax.ShapeDtypeStruct((depth, COND_ROWS, d3), F32),
        grid=(depth, nt),
        in_specs=[pl.BlockSpec((COND_ROWS, d), lambda i, j: (0, 0)),
                  pl.BlockSpec((1, d, d), lambda i, j: (i, 0, j)),
                  pl.BlockSpec((1, 1, d), lambda i, j: (i, 0, j))],
        out_specs=pl.BlockSpec((1, COND_ROWS, d), lambda i, j: (i, 0, j)),
        compiler_params=_params(2), name="ada",
    )(cond, ada_w, ada_b.reshape(depth, 1, d3))


def _rope_slab(t, cos, sin_lo, sin_hi):
    return (t * cos + pltpu.roll(t, V7X_LANES - 16, axis=1) * sin_lo
            + pltpu.roll(t, 16, axis=1) * sin_hi)


def _proj0_kernel(x_ref, ng_ref, sh_ref, sc_ref, cos_ref, slo_ref, shi_ref, w_ref, wvt_ref,
                  q_ref, k_ref, vt_ref, p_ref, g_ref, *, d, q_scale):
    hb = _modulate(x_ref[0], ng_ref[...], sh_ref[0], sc_ref[0]).astype(BF16)
    cos, slo, shi = cos_ref[...], slo_ref[...], shi_ref[...]

    def proj(c0, width):
        return jnp.dot(hb, w_ref[:, c0:c0 + width], preferred_element_type=F32)

    def rope(t, scale):
        for s in range(d // V7X_LANES):
            sl = slice(s * V7X_LANES, (s + 1) * V7X_LANES)
            yield sl, _rope_slab(t[:, sl], cos, slo, shi) * scale

    for sl, r in rope(proj(0, d), q_scale):
        q_ref[0, :, sl] = r.astype(BF16)
    for sl, r in rope(proj(d, d), 1.0):
        k_ref[0, :, sl] = r.astype(BF16)
    vt_ref[0] = _nt_dot(wvt_ref[...], hb).astype(BF16)
    p_ref[0] = proj(3 * d, d)
    g_ref[0] = proj(4 * d, 2 * d)


def _proj0(x, norm_g, shift, scale, cos, sin_lo, sin_hi, w_bf16, wvt_bf16):
    b, n, d = x.shape
    ncols = w_bf16.shape[1]
    tl = ROW_TILE
    row = lambda j, i: (i, j, 0)
    vec = lambda j, i: (i, 0, 0)
    tab = lambda j, i: (j, 0)
    kern = functools.partial(_proj0_kernel, d=d, q_scale=A_HEAD_DIM ** -0.5 * LOG2E)
    return pl.pallas_call(
        kern,
        out_shape=(jax.ShapeDtypeStruct((b, n, d), BF16), jax.ShapeDtypeStruct((b, n, d), BF16),
                   jax.ShapeDtypeStruct((b, d, n), BF16), jax.ShapeDtypeStruct((b, n, d), F32),
                   jax.ShapeDtypeStruct((b, n, 2 * d), F32)),
        grid=(n // tl, b),
        in_specs=[pl.BlockSpec((1, tl, d), row),
                  pl.BlockSpec((1, d), lambda j, i: (0, 0)),
                  pl.BlockSpec((1, 1, d), vec), pl.BlockSpec((1, 1, d), vec),
                  pl.BlockSpec((tl, V7X_LANES), tab), pl.BlockSpec((tl, V7X_LANES), tab),
                  pl.BlockSpec((tl, V7X_LANES), tab),
                  pl.BlockSpec((d, ncols), lambda j, i: (0, 0), pipeline_mode=pl.Buffered(1)),
                  pl.BlockSpec((d, d), lambda j, i: (0, 0), pipeline_mode=pl.Buffered(1))],
        out_specs=(pl.BlockSpec((1, tl, d), row), pl.BlockSpec((1, tl, d), row),
                   pl.BlockSpec((1, d, tl), lambda j, i: (i, 0, j)), pl.BlockSpec((1, tl, d), row),
                   pl.BlockSpec((1, tl, 2 * d), row)),
        compiler_params=_params(2), name="proj0",
    )(x, norm_g, shift, scale, cos, sin_lo, sin_hi, w_bf16, wvt_bf16)


def _ctx_kv_kernel(c_ref, ng_ref, sh_ref, sc_ref, wk_ref, wvt_ref, k_ref, vt_ref):
    hb = _modulate(c_ref[...], ng_ref[...], sh_ref[...], sc_ref[...]).astype(BF16)
    k_ref[...] = jnp.dot(hb, wk_ref[...], preferred_element_type=F32).astype(BF16)
    vt_ref[...] = _nt_dot(wvt_ref[...], hb).astype(BF16)


def _ctx_kv(ctx_rows, norm_g, shift_c, scale_c, w_bf16, wvt_bf16):
    rows, d = ctx_rows.shape
    tr = ROW_TILE
    one = lambda i: (0, 0)
    return pl.pallas_call(
        _ctx_kv_kernel,
        out_shape=(jax.ShapeDtypeStruct((rows, d), BF16), jax.ShapeDtypeStruct((d, rows), BF16)),
        grid=(rows // tr,),
        in_specs=[pl.BlockSpec((tr, d), lambda i: (i, 0)),
                  pl.BlockSpec((1, d), one), pl.BlockSpec((1, d), one), pl.BlockSpec((1, d), one),
                  pl.BlockSpec((d, d), lambda i: (0, 1)), pl.BlockSpec((d, d), one)],
        out_specs=(pl.BlockSpec((tr, d), lambda i: (i, 0)), pl.BlockSpec((d, tr), lambda i: (0, i))),
        compiler_params=_params(1), name="ctx_kv",
    )(ctx_rows, norm_g, shift_c, scale_c, w_bf16, wvt_bf16)


def _attn_kernel(q_ref, kc_ref, k_ref, vct_ref, vt_ref, lq1_ref, lk1_ref, lq2_ref, lk2_ref, sg_ref,
                 o_ref, s_ref, *, lam_init):
    n_sub = q_ref.shape[1] // Q_SUB
    n_chunk = (kc_ref.shape[1] + k_ref.shape[1]) // KEY_CHUNK
    n_ctx = kc_ref.shape[1] // KEY_CHUNK
    groups = KEY_CHUNK // V7X_SUBLANES
    lam = (jnp.exp(jnp.sum(lq1_ref[...] * lk1_ref[...], axis=-1, keepdims=True))
           - jnp.exp(jnp.sum(lq2_ref[...] * lk2_ref[...], axis=-1, keepdims=True)) + lam_init)
    lane = lax.broadcasted_iota(jnp.int32, (Q_SUB, A_V_DIM), 1)
    masks = (lane < A_HEAD_DIM, lane >= A_HEAD_DIM)

    def keys(j):
        if j < n_ctx:
            return kc_ref[0, j * KEY_CHUNK:(j + 1) * KEY_CHUNK, :]
        return k_ref[0, (j - n_ctx) * KEY_CHUNK:(j - n_ctx + 1) * KEY_CHUNK, :]

    def values_t(j):
        if j < n_ctx:
            return vct_ref[:, j * KEY_CHUNK:(j + 1) * KEY_CHUNK]
        return vt_ref[0, :, (j - n_ctx) * KEY_CHUNK:(j - n_ctx + 1) * KEY_CHUNK]

    def fold(x, op):
        return op(x.reshape(groups, V7X_SUBLANES, Q_SUB), axis=0)

    state = {}

    def score_task(t, c, j):
        if (t, c, "q") not in state:
            q = q_ref[0, t * Q_SUB:(t + 1) * Q_SUB, :]
            state[t, c, "q"] = jnp.where(masks[c], q, jnp.zeros_like(q))
        s = _nt_dot(keys(j), state[t, c, "q"])
        s_ref[t % 2, c, j * KEY_CHUNK:(j + 1) * KEY_CHUNK, :] = s
        m8 = fold(s, jnp.max)
        state[t, c, "m8"] = m8 if j == 0 else jnp.maximum(state[t, c, "m8"], m8)

    def exp_task(t, c, j):
        if j == 0:
            state[t, c, "m"] = jnp.max(state[t, c, "m8"], axis=0, keepdims=True)
        p = jnp.exp2(s_ref[t % 2, c, j * KEY_CHUNK:(j + 1) * KEY_CHUNK, :] - state[t, c, "m"])
        l8 = fold(p, jnp.sum)
        pv = jnp.dot(values_t(j), p.astype(BF16), preferred_element_type=F32)
        state[t, c, "l8"] = l8 if j == 0 else state[t, c, "l8"] + l8
        state[t, c, "acc"] = pv if j == 0 else state[t, c, "acc"] + pv

    def finish(t):
        l1 = jnp.sum(state[t, 0, "l8"], axis=0, keepdims=True)
        l2 = jnp.sum(state[t, 1, "l8"], axis=0, keepdims=True)
        ot = state[t, 0, "acc"] * (1.0 / l1) - state[t, 1, "acc"] * (lam / l2)
        on = ot * lax.rsqrt(jnp.mean(ot * ot, axis=0, keepdims=True) + SUBLN_EPS)
        o_ref[0, t * Q_SUB:(t + 1) * Q_SUB, :] = (on.T * sg_ref[...]) * (1.0 - lam_init)

    order = [(c, j) for c in range(2) for j in range(n_chunk)]
    for t in range(n_sub + 1):
        for c, j in order:
            if t < n_sub:
                score_task(t, c, j)
            if t > 0:
                exp_task(t - 1, c, j)
        if t > 0:
            finish(t - 1)


def _attention(q, kc, k, vct, vt, lq1, lk1, lq2, lk2, subln_g, lam_init):
    b, n, d = q.shape
    nc = kc.shape[1]
    hd = A_V_DIM
    tq = Q_TILE
    kv = lambda bi, h, i: (bi, 0, h)
    one = lambda bi, h, i: (0, 0)
    kern = functools.partial(_attn_kernel, lam_init=lam_init)
    return pl.pallas_call(
        kern,
        out_shape=jax.ShapeDtypeStruct((b, n, d), F32),
        grid=(b, A_HEADS, n // tq),
        in_specs=[pl.BlockSpec((1, tq, hd), lambda bi, h, i: (bi, i, h)),
                  pl.BlockSpec((1, nc, hd), kv), pl.BlockSpec((1, n, hd), kv),
                  pl.BlockSpec((hd, nc), lambda bi, h, i: (h, bi)),
                  pl.BlockSpec((1, hd, n), lambda bi, h, i: (bi, h, 0)),
                  pl.BlockSpec((1, A_HEAD_DIM), one), pl.BlockSpec((1, A_HEAD_DIM), one),
                  pl.BlockSpec((1, A_HEAD_DIM), one), pl.BlockSpec((1, A_HEAD_DIM), one),
                  pl.BlockSpec((1, hd), one)],
        out_specs=pl.BlockSpec((1, tq, hd), lambda bi, h, i: (bi, i, h)),
        scratch_shapes=[pltpu.VMEM((2, 2, nc + n, Q_SUB), F32)],
        compiler_params=_params(3), name="diff_attn",
    )(q, kc, k, vct, vt, lq1, lk1, lq2, lk2, subln_g)


def _pool_kernel(p_ref, pw_ref, ps_ref, o_ref, pad_ref, *, n, group):
    rows = ROW_TILE
    zeros = jnp.zeros((POOL_HALO, group), F32)
    for gi, win in enumerate(POOL_WINDOWS):
        sl = slice(gi * group, (gi + 1) * group)
        pad_ref[0:POOL_HALO, :] = zeros
        pad_ref[POOL_HALO + n:2 * POOL_HALO + n, :] = zeros
        pad_ref[POOL_HALO:POOL_HALO + n, :] = p_ref[0, :, sl]
        back = win // 2
        for r0 in range(0, n, rows):
            acc = pad_ref[pl.ds(POOL_HALO + r0 - back, rows), :]
            for j in range(1 - back, win - back):
                acc = acc + pad_ref[pl.ds(POOL_HALO + r0 + j, rows), :]
            t = r0 + lax.broadcasted_iota(jnp.int32, (rows, 1), 0)
            cnt = (jnp.minimum(t + (win - back), n) - jnp.maximum(t - back, 0)).astype(F32)
            m = acc / cnt - pad_ref[pl.ds(POOL_HALO + r0, rows), :]
            y = jnp.dot(m.astype(BF16), pw_ref[gi], preferred_element_type=F32)
            o_ref[0, r0:r0 + rows, sl] = y * ps_ref[:, sl]


def _pool(p, pool_w_bf16, pool_scale):
    b, n, width = p.shape
    ng, group, _ = pool_w_bf16.shape
    kern = functools.partial(_pool_kernel, n=n, group=group)
    return pl.pallas_call(
        kern,
        out_shape=jax.ShapeDtypeStruct((b, n, width), F32),
        grid=(b,),
        in_specs=[pl.BlockSpec((1, n, width), lambda i: (i, 0, 0)),
                  pl.BlockSpec((ng, group, group), lambda i: (0, 0, 0)),
                  pl.BlockSpec((1, width), lambda i: (0, 0))],
        out_specs=pl.BlockSpec((1, n, width), lambda i: (i, 0, 0)),
        scratch_shapes=[pltpu.VMEM((n + 2 * POOL_HALO, group), F32)],
        compiler_params=_params(1), name="pool",
    )(p, pool_w_bf16, pool_scale)


def _out0_kernel(o_ref, yp_ref, g_ref, x_ref, gate_ref, w_ref, ng_ref, sh_ref, sc_ref,
                 x1_ref, h1_ref, *, aw):
    sg = _silu(g_ref[0])
    a = (o_ref[0] * sg[:, :aw]).astype(BF16)
    c = (yp_ref[0] * sg[:, aw:]).astype(BF16)
    y = (jnp.dot(a, w_ref[0:aw, :], preferred_element_type=F32)
         + jnp.dot(c, w_ref[aw:, :], preferred_element_type=F32))
    x1 = x_ref[0] + gate_ref[0] * y
    x1_ref[0] = x1
    h1_ref[0] = _modulate(x1, ng_ref[...], sh_ref[0], sc_ref[0]).astype(BF16)


def _out0(o, ypool, g, x, gate, w_bf16, norm_g1, shift1, scale1):
    b, n, d = x.shape
    aw = o.shape[2]
    e = g.shape[2]
    tl = ROW_TILE
    row = lambda i, j: (i, j, 0)
    vec = lambda i, j: (i, 0, 0)
    kern = functools.partial(_out0_kernel, aw=aw)
    return pl.pallas_call(
        kern,
        out_shape=(jax.ShapeDtypeStruct((b, n, d), F32), jax.ShapeDtypeStruct((b, n, d), BF16)),
        grid=(b, n // tl),
        in_specs=[pl.BlockSpec((1, tl, aw), row), pl.BlockSpec((1, tl, e - aw), row),
                  pl.BlockSpec((1, tl, e), row), pl.BlockSpec((1, tl, d), row),
                  pl.BlockSpec((1, 1, d), vec),
                  pl.BlockSpec((e, d), lambda i, j: (0, 0)),
                  pl.BlockSpec((1, d), lambda i, j: (0, 0)),
                  pl.BlockSpec((1, 1, d), vec), pl.BlockSpec((1, 1, d), vec)],
        out_specs=(pl.BlockSpec((1, tl, d), row), pl.BlockSpec((1, tl, d), row)),
        compiler_params=_params(2), name="out0",
    )(o, ypool, g, x, gate, w_bf16, norm_g1, shift1, scale1)


def _filter_mlp_kernel(z_ref, w0, b0, f0, w1, b1, f1, w2, b2, f2, o_ref):
    def layer(h, w, b, f):
        return jnp.sin(f[...] * (jnp.dot(h, w[...], preferred_element_type=F32, precision=HIGHEST) + b[...]))
    h = layer(z_ref[...], w0, b0, f0)
    h = layer(h, w1, b1, f1)
    o_ref[...] = layer(h, w2, b2, f2)


def _filter_mlp(z, w0, b0, f0, w1, b1, f1, w2, b2, f2):
    return pl.pallas_call(
        _filter_mlp_kernel,
        out_shape=jax.ShapeDtypeStruct(z.shape, F32),
        compiler_params=pltpu.CompilerParams(vmem_limit_bytes=VMEM_LIMIT_BYTES), name="filter_mlp",
    )(z, w0, b0, f0, w1, b1, f1, w2, b2, f2)


def _aligned(start, multiple):
    return start if isinstance(start, int) else pl.multiple_of(start, multiple)


def _row_chunks(n, body, init=None, unrolled=False):
    if unrolled:
        carry = init
        for c in range(n // CONV_ROWS):
            carry = body(c * CONV_ROWS, carry)
        return carry

    def step(c, carry):
        return body(pl.multiple_of(c * CONV_ROWS, CONV_ROWS), carry)
    return lax.fori_loop(0, n // CONV_ROWS, step, init)


def _dft8_half(x0, x2, x4, x6):
    s04, d04, s26, d26 = x0 + x4, x0 - x4, x2 + x6, x2 - x6
    cd, cs = SQRT_HALF * d26, SQRT_HALF * s26
    re = (s04 + s26, x0 + cd, d04, x0 - cd, s04 - s26)
    im = (None, -(x4 + cs), -d26, x4 - cs, None)
    return re, im


def _fft16_forward(x):
    er, ei = _dft8_half(x[0], x[2], x[4], x[6])
    orr, oi = _dft8_half(x[1], x[3], x[5], x[7])
    tr, ti = [None] * FFT_RES, [None] * FFT_RES
    tr[0] = er[0] + orr[0]
    tr[8] = er[0] - orr[0]
    tr[4], ti[4] = er[4], -orr[4]
    for k in (1, 2, 3):
        wr, wi = math.cos(math.pi * k / 8), -math.sin(math.pi * k / 8)
        pr = wr * orr[k] - wi * oi[k]
        pi = wr * oi[k] + wi * orr[k]
        tr[k], ti[k] = er[k] + pr, ei[k] + pi
        tr[8 - k], ti[8 - k] = er[k] - pr, pi - ei[k]
    return tr, ti


def _fft16_inverse(hr, hi):
    def quad(a0, a4, a1, a2, a3):
        (a1r, a1i), (a2r, a2i), (a3r, a3i) = a1, a2, a3
        lo, hi_ = a0 + a4, a0 - a4
        return (lo + 2.0 * (a1r + a2r + a3r),
                hi_ + 2.0 * (SQRT_HALF * ((a1r - a1i) - (a3r + a3i)) - a2i),
                lo + 2.0 * (a3i - a1i - a2r),
                hi_ + 2.0 * (SQRT_HALF * ((a3r - a3i) - (a1r + a1i)) + a2i))
    even = quad(hr[0] + hr[8], 2.0 * hr[4], *[(hr[k] + hr[8 - k], hi[k] - hi[8 - k]) for k in (1, 2, 3)])
    odd_in = []
    for k in (1, 2, 3):
        dr, di = hr[k] - hr[8 - k], hi[k] + hi[8 - k]
        wr, wi = math.cos(math.pi * k / 8), math.sin(math.pi * k / 8)
        odd_in.append((dr * wr - di * wi, dr * wi + di * wr))
    odd = quad(hr[0] - hr[8], -2.0 * hi[4], *odd_in)
    return [even[0], odd[0], even[1], odd[1], even[2], odd[2], even[3], odd[3]]


def _fft_stage1(u_ref, tw_ref, side_work=None):
    nb = FFT_BLOCK
    per_step = (nb // FFT_ROWS) // (nb * (FFT_RADIX // 2) // CONV_ROWS)

    def chunk(a0):
        x = [u_ref[pl.ds(_aligned(b * nb + a0, FFT_ROWS), FFT_ROWS), :] for b in range(FFT_RADIX // 2)]
        tr, ti = _fft16_forward(x)
        re_rows = pl.ds(a0, FFT_ROWS)
        im_rows = pl.ds(_aligned(nb + a0, FFT_ROWS), FFT_ROWS)
        for k in range(FFT_RES):
            tw_ref[k, re_rows, :] = tr[k].astype(BF16)
            if ti[k] is not None:
                tw_ref[k, im_rows, :] = ti[k].astype(BF16)

    if side_work is not None:
        for i in range(nb // (FFT_ROWS * per_step)):
            for j in range(per_step):
                chunk((i * per_step + j) * FFT_ROWS)
            side_work(i * CONV_ROWS)
        return

    def body(i, carry):
        chunk(pl.multiple_of(i * FFT_ROWS, FFT_ROWS))
        return carry

    lax.fori_loop(0, nb // FFT_ROWS, body, None)


def _fft_forward_mxu(tw_ref, dfw_ref, k):
    if k in (0, FFT_RES - 1):
        return jnp.dot(dfw_ref[k, :, 0:FFT_BLOCK], tw_ref[k, 0:FFT_BLOCK, :], preferred_element_type=F32)
    return jnp.dot(dfw_ref[k], tw_ref[k], preferred_element_type=F32)


def _fft_stage2_inverse(h_ref, y_ref):
    nb = FFT_BLOCK

    def body(i, carry):
        a0 = pl.multiple_of(i * V7X_SUBLANES, V7X_SUBLANES)
        re_rows = pl.ds(a0, V7X_SUBLANES)
        im_rows = pl.ds(pl.multiple_of(nb + a0, V7X_SUBLANES), V7X_SUBLANES)
        hr = [h_ref[k, re_rows, :] for k in range(FFT_RES)]
        hi = [None] + [h_ref[k, im_rows, :] for k in range(1, FFT_RES - 1)] + [None]
        for b, yb in enumerate(_fft16_inverse(hr, hi)):
            y_ref[pl.ds(pl.multiple_of(b * nb + a0, V7X_SUBLANES), V7X_SUBLANES), :] = yb
        return carry

    lax.fori_loop(0, nb // V7X_SUBLANES, body, None)


def _filter_spec_kernel(h_ref, wf_ref, wb_ref, t_ref, dcy_ref, dfw_ref,
                        gr_ref, gi_ref, fwd_ref, bwd_ref, twf_ref, twb_ref, *, n):
    inv_n = 1.0 / (2 * n)
    row = lax.broadcasted_iota(jnp.int32, (CONV_ROWS, CH_TILE), 0)

    def taps(r0, carry):
        rows = pl.ds(r0, CONV_ROWS)
        hm = h_ref[rows, :]
        decay = jnp.exp(-t_ref[rows, :] * dcy_ref[...])
        fwd_ref[rows, :] = jnp.dot(hm, wf_ref[...], preferred_element_type=F32, precision=HIGHEST) * decay
        bwd = jnp.dot(hm, wb_ref[...], preferred_element_type=F32, precision=HIGHEST) * decay
        bwd_ref[rows, :] = jnp.where(row + r0 == 0, 0.0, bwd)
        return carry

    _row_chunks(n, taps)
    _fft_stage1(fwd_ref, twf_ref)
    _fft_stage1(bwd_ref, twb_ref)
    nb = FFT_BLOCK
    for k in range(FFT_RES):
        xf = _fft_forward_mxu(twf_ref, dfw_ref, k)
        xb = _fft_forward_mxu(twb_ref, dfw_ref, k)
        gr_ref[0, k] = (xf[0:nb] + xb[0:nb]) * inv_n
        gi_ref[0, k] = (xf[nb:2 * nb] - xb[nb:2 * nb]) * inv_n


def _filter_spec(hmlp, wout, tcol, dabs, dfw):
    n, kpad = hmlp.shape
    e = dabs.shape[1]
    ct = CH_TILE
    nct = e // ct
    nb = FFT_BLOCK
    const = lambda o, j: (0, 0)
    gshape = jax.ShapeDtypeStruct((H_ORDER, FFT_RES, nb, e), F32)
    gspec = pl.BlockSpec((1, FFT_RES, nb, ct), lambda o, j: (o, 0, 0, j))
    kern = functools.partial(_filter_spec_kernel, n=n)
    return pl.pallas_call(
        kern,
        out_shape=(gshape, gshape),
        grid=(H_ORDER, nct),
        in_specs=[pl.BlockSpec((n, kpad), const),
                  pl.BlockSpec((kpad, ct), lambda o, j: (0, (2 * o) * nct + j)),
                  pl.BlockSpec((kpad, ct), lambda o, j: (0, (2 * o + 1) * nct + j)),
                  pl.BlockSpec((n, 1), const),
                  pl.BlockSpec((1, ct), lambda o, j: (0, j)),
                  pl.BlockSpec((FFT_RES, 2 * nb, 2 * nb), lambda o, j: (0, 0, 0), pipeline_mode=pl.Buffered(1))],
        out_specs=(gspec, gspec),
        scratch_shapes=[pltpu.VMEM((n, ct), F32), pltpu.VMEM((n, ct), F32),
                        pltpu.VMEM((FFT_RES, 2 * nb, ct), BF16), pltpu.VMEM((FFT_RES, 2 * nb, ct), BF16)],
        compiler_params=_params(2), name="filter_spec",
    )(hmlp, wout, wout, tcol, dabs, dfw)


def _hyena_kernel(h_ref, w_ref, cwv_ref, cw1_ref, cw2_ref,
                  cbv_ref, cb1_ref, cb2_ref, dfw_ref, dinv_ref,
                  gr0_ref, gi0_ref, gr1_ref, gi1_ref, fb0_ref, fb1_ref,
                  o_ref, rawa_ref, rawb_ref, u_ref, tw_ref, res_ref, y_ref, *, n):
    halo = CONV_HALO
    nb = FFT_BLOCK
    ct = CH_TILE
    zero_halo = jnp.zeros((halo, 2 * ct), F32)
    for raw_ref in (rawa_ref, rawb_ref):
        raw_ref[0:halo, :] = zero_halo
        raw_ref[halo + n:2 * halo + n, :] = zero_halo

    def project_chunk(raw_ref, pair, r0):
        raw_ref[pl.ds(halo + r0, CONV_ROWS), :] = jnp.dot(
            h_ref[0, pl.ds(r0, CONV_ROWS), :], w_ref[:, 2 * pair * ct:2 * (pair + 1) * ct],
            preferred_element_type=F32)

    def short_conv(raw_ref, half, r0, cw_ref, cb_ref):
        blk = raw_ref[pl.ds(r0, CONV_ROWS + 2 * halo), half * ct:(half + 1) * ct]
        prev = blk[halo - 1:halo - 1 + CONV_ROWS]
        cur = blk[halo:halo + CONV_ROWS]
        nxt = blk[halo + 1:halo + 1 + CONV_ROWS]
        return cb_ref[...] + prev * cw_ref[0:1, :] + cur * cw_ref[1:2, :] + nxt * cw_ref[2:3, :]

    def long_conv(gr_ref, gi_ref, side_work=None):
        _fft_stage1(u_ref, tw_ref, side_work)
        for k in range(FFT_RES):
            x = _fft_forward_mxu(tw_ref, dfw_ref, k)
            xr, xi = x[0:nb], x[nb:2 * nb]
            gr, gi = gr_ref[0, k], gi_ref[0, k]
            y = jnp.concatenate([xr * gr - xi * gi, xr * gi + xi * gr], axis=0).astype(BF16)
            if k in (0, FFT_RES - 1):
                res_ref[k, 0:nb, :] = jnp.dot(dinv_ref[k, 0:nb, :], y, preferred_element_type=F32)
            else:
                res_ref[k] = jnp.dot(dinv_ref[k], y, preferred_element_type=F32)
        _fft_stage2_inverse(res_ref, y_ref)

    def mixed(raw_ref, half, r0, cw_ref, cb_ref, fb_ref):
        rows = pl.ds(r0, CONV_ROWS)
        return short_conv(raw_ref, half, r0, cw_ref, cb_ref) * (y_ref[rows, :] + u_ref[rows, :] * fb_ref[0])

    def store_u(r0, u):
        u_ref[pl.ds(r0, CONV_ROWS), :] = u

    _row_chunks(n, lambda r0, c: project_chunk(rawa_ref, 0, r0), unrolled=True)
    _row_chunks(n, lambda r0, c: store_u(r0, short_conv(rawa_ref, 0, r0, cwv_ref, cbv_ref)))
    long_conv(gr0_ref, gi0_ref, side_work=lambda r0: project_chunk(rawb_ref, 1, r0))
    _row_chunks(n, lambda r0, c: store_u(r0, mixed(rawa_ref, 1, r0, cw1_ref, cb1_ref, fb0_ref)))
    long_conv(gr1_ref, gi1_ref)

    def finish(r0, carry):
        z = mixed(rawb_ref, 0, r0, cw2_ref, cb2_ref, fb1_ref)
        gate = rawb_ref[pl.ds(halo + r0, CONV_ROWS), ct:2 * ct]
        o_ref[0, pl.ds(r0, CONV_ROWS), :] = (z * _silu(gate)).astype(BF16)
        return carry
    _row_chunks(n, finish)


def _hyena(h1, w_tiles_bf16, conv_w, conv_b, dfw, dinv, gr, gi, fbias):
    b, n, d = h1.shape
    e = gr.shape[3]
    ct = CH_TILE
    nct = e // ct
    nb = FFT_BLOCK
    cwspec = lambda k: pl.BlockSpec((SHORT_CONV, ct), lambda j, i, k=k: (0, k * nct + j))
    cbspec = lambda k: pl.BlockSpec((1, ct), lambda j, i, k=k: (0, k * nct + j))
    dspec = pl.BlockSpec((FFT_RES, 2 * nb, 2 * nb), lambda j, i: (0, 0, 0), pipeline_mode=pl.Buffered(1))
    gspec = lambda o: pl.BlockSpec((1, FFT_RES, nb, ct), lambda j, i, o=o: (o, 0, 0, j),
                                   pipeline_mode=pl.Buffered(1))
    vspec = lambda o: pl.BlockSpec((1, 1, ct), lambda j, i, o=o: (o, 0, j))
    kern = functools.partial(_hyena_kernel, n=n)
    return pl.pallas_call(
        kern,
        out_shape=jax.ShapeDtypeStruct((b, n, e), BF16),
        grid=(nct, b),
        in_specs=[pl.BlockSpec((1, n, d), lambda j, i: (i, 0, 0)),
                  pl.BlockSpec((d, 4 * ct), lambda j, i: (0, j)),
                  cwspec(0), cwspec(1), cwspec(2), cbspec(0), cbspec(1), cbspec(2),
                  dspec, dspec,
                  gspec(0), gspec(0), gspec(1), gspec(1), vspec(0), vspec(1)],
        out_specs=pl.BlockSpec((1, n, ct), lambda j, i: (i, 0, j)),
        scratch_shapes=[pltpu.VMEM((n + 2 * CONV_HALO, 2 * ct), F32), pltpu.VMEM((n + 2 * CONV_HALO, 2 * ct), F32),
                        pltpu.VMEM((n, ct), F32),
                        pltpu.VMEM((FFT_RES, 2 * nb, ct), BF16), pltpu.VMEM((FFT_RES, 2 * nb, ct), F32),
                        pltpu.VMEM((n, ct), F32)],
        compiler_params=_params(2), name="hyena_mix",
    )(h1, w_tiles_bf16, conv_w, conv_w, conv_w, conv_b, conv_b, conv_b,
      dfw, dinv, gr, gi, gr, gi, fbias, fbias)


def _out1_kernel(z_ref, x_ref, gate_ref, w_ref, fg_ref, o_ref):
    y = jnp.dot(z_ref[0], w_ref[...], preferred_element_type=F32)
    x2 = x_ref[0] + gate_ref[0] * y
    o_ref[0] = (x2 * lax.rsqrt(jnp.mean(x2 * x2, axis=-1, keepdims=True) + RMS_EPS)) * fg_ref[...]


def _out1(zg, x1, gate, w_bf16, final_g):
    b, n, d = x1.shape
    e = zg.shape[2]
    tl = ROW_TILE
    row = lambda i, j: (i, j, 0)
    return pl.pallas_call(
        _out1_kernel,
        out_shape=jax.ShapeDtypeStruct((b, n, d), F32),
        grid=(b, n // tl),
        in_specs=[pl.BlockSpec((1, tl, e), row), pl.BlockSpec((1, tl, d), row),
                  pl.BlockSpec((1, 1, d), lambda i, j: (i, 0, 0)),
                  pl.BlockSpec((e, d), lambda i, j: (0, 0)),
                  pl.BlockSpec((1, d), lambda i, j: (0, 0))],
        out_specs=pl.BlockSpec((1, tl, d), row),
        compiler_params=_params(2), name="out1",
    )(zg, x1, gate, w_bf16, final_g)


def _rope_tables(n):
    rows = n // GRID_W
    row = jnp.repeat(jnp.arange(rows), GRID_W).astype(F32)
    col = jnp.tile(jnp.arange(GRID_W), rows).astype(F32)
    half = A_HEAD_DIM // 2
    inv = ROPE_THETA ** (-jnp.arange(0, half, 2, dtype=F32) / half)
    ar, ac = row[:, None] * inv, col[:, None] * inv
    ang = jnp.concatenate([ar, ar, ac, ac], axis=-1)
    cos, sin = jnp.cos(ang), jnp.sin(ang)
    first = (jnp.arange(A_HEAD_DIM) % (half)) < (half // 2)
    sin_lo = jnp.where(first, -sin, 0.0)
    sin_hi = jnp.where(first, 0.0, sin)
    rep = V7X_LANES // A_HEAD_DIM
    return jnp.tile(cos, (1, rep)), jnp.tile(sin_lo, (1, rep)), jnp.tile(sin_hi, (1, rep))


def _fft_tables(n):
    nb = FFT_BLOCK
    idx = jnp.arange(nb, dtype=jnp.int32)
    k2 = jnp.arange(FFT_RES, dtype=jnp.int32)
    freq = FFT_RADIX * idx[None, :, None] + k2[:, None, None]
    ang = ((freq * idx[None, None, :]) % (2 * n)).astype(F32) * (math.pi / n)
    c, s = jnp.cos(ang), jnp.sin(ang)
    dfw = jnp.concatenate([jnp.concatenate([c, s], axis=2), jnp.concatenate([-s, c], axis=2)], axis=1)
    return dfw.astype(BF16), jnp.swapaxes(dfw, 1, 2).astype(BF16)


def _filter_features(n, kpad):
    t = jnp.linspace(0.0, 1.0, n, dtype=F32)[:, None]
    w = 2.0 * math.pi * jnp.arange(n, dtype=F32)[:, None] / n
    bands = jnp.linspace(1e-4, FILTER_BANDS - 1, FILTER_BANDS, dtype=F32)[None, :]
    z = jnp.concatenate([t, jnp.cos(bands * w), -jnp.sin(bands * w)], axis=-1)
    return jnp.pad(z, ((0, 0), (0, kpad - z.shape[1]))), t


def _pad2(a, rows, cols):
    return jnp.pad(a, ((0, rows - a.shape[0]), (0, cols - a.shape[1])))


def kernel(x, c, ctx, c_ctx, norm_g, ada_w, ada_b, final_g, a_w_in, a_lam_q1, a_lam_k1, a_lam_q2, a_lam_k2, a_subln_g, a_pool_w, a_pool_scale, a_w_out, h_w_in, h_conv_w, h_conv_b, h_filt_w0, h_filt_b0, h_filt_f0, h_filt_w1, h_filt_b1, h_filt_f1, h_filt_w2, h_filt_b2, h_filt_f2, h_filt_wout, h_filt_bias, h_w_out):
    b, n, d = x.shape
    nc = ctx.shape[1]
    e = h_w_out.shape[1]
    assert norm_g.shape[0] == 2 and a_w_in.shape[0] == 1 and h_w_in.shape[0] == 1
    assert b + 1 <= COND_ROWS and n % ROW_TILE == 0 and n % Q_TILE == 0 and (b * nc) % ROW_TILE == 0
    assert d == A_HEADS * A_V_DIM and e % CH_TILE == 0 and 2 * n == FFT_BLOCK * FFT_RADIX

    cond = jnp.concatenate([c, c_ctx[None, :], jnp.zeros((COND_ROWS - b - 1, d), F32)], axis=0)
    mods = _ada(cond, ada_w, ada_b)
    shift = mods[:, :, 0:d]
    scale = mods[:, :, d:2 * d]
    gate = mods[:, :, 2 * d:3 * d]
    per_batch = lambda m, i: m[i, :b].reshape(b, 1, d)

    lam_init = 0.8 - 0.6 * math.exp(-0.3 * 0)
    w0 = a_w_in[0].astype(BF16)
    cos, sin_lo, sin_hi = _rope_tables(n)
    wvt = a_w_in[0, :, 2 * d:3 * d].T.astype(BF16)
    q, k, vt, p, g = _proj0(x, norm_g[0:1], per_batch(shift, 0), per_batch(scale, 0), cos, sin_lo, sin_hi,
                            w0, wvt)
    kc, vct = _ctx_kv(ctx.reshape(b * nc, d), norm_g[0:1], shift[0, b:b + 1], scale[0, b:b + 1], w0, wvt)
    o = _attention(q, kc.reshape(b, nc, d), k, vct, vt,
                   a_lam_q1, a_lam_k1, a_lam_q2, a_lam_k2, a_subln_g, lam_init)
    ypool = _pool(p, a_pool_w[0].astype(BF16), a_pool_scale)
    x1, h1 = _out0(o, ypool, g, x, per_batch(gate, 0), a_w_out[0].astype(BF16),
                   norm_g[1:2], per_batch(shift, 1), per_batch(scale, 1))

    kpad = V7X_LANES
    z, tcol = _filter_features(n, kpad)
    row1 = lambda a: _pad2(a, 1, kpad)
    hmlp = _filter_mlp(z, _pad2(h_filt_w0[0], kpad, kpad), row1(h_filt_b0), row1(h_filt_f0),
                       _pad2(h_filt_w1[0], kpad, kpad), row1(h_filt_b1), row1(h_filt_f1),
                       _pad2(h_filt_w2[0], kpad, kpad), row1(h_filt_b2), row1(h_filt_f2))
    max_decay = math.log(DECAY_TARGET) / FAST_DECAY_PCT
    min_decay = math.log(DECAY_TARGET) / SLOW_DECAY_PCT
    dabs = jnp.abs(jnp.linspace(min_decay, max_decay, e, dtype=F32))[None, :]
    dfw, dinv = _fft_tables(n)
    wout = _pad2(h_filt_wout[0], kpad, h_filt_wout.shape[2])
    gr, gi = _filter_spec(hmlp, wout, tcol, dabs, dfw)
    nct = e // CH_TILE
    w_tiles = (h_w_in[0].astype(BF16).reshape(d, H_ORDER + 2, nct, CH_TILE)
               .transpose(0, 2, 1, 3).reshape(d, (H_ORDER + 2) * e))
    zg = _hyena(h1, w_tiles, h_conv_w[0], h_conv_b, dfw, dinv, gr, gi,
                h_filt_bias[0].reshape(H_ORDER, 1, e))
    return _out1(zg, x1, per_batch(gate, 1), h_w_out[0].astype(BF16), final_g[None, :])
```

```python
import functools
import math

import jax
import jax.numpy as jnp
from jax import lax
from jax.experimental import pallas as pl
from jax.experimental.pallas import tpu as pltpu

F32 = jnp.float32
BF16 = jnp.bfloat16
HIGHEST = lax.Precision.HIGHEST

GRID_W = 64
A_HEADS = 8
A_HEAD_DIM = 64
A_V_DIM = 2 * A_HEAD_DIM
POOL_WINDOWS = (2, 4, 8, 16)
ROPE_THETA = 10000.0
H_ORDER = 2
SHORT_CONV = 3
FILTER_EMB = 33
FILTER_BANDS = (FILTER_EMB - 1) // 2
DECAY_TARGET = 1e-2
FAST_DECAY_PCT = 0.3
SLOW_DECAY_PCT = 1.5
RMS_EPS = 1e-6
SUBLN_EPS = 1e-5
LOG2E = 1.4426950408889634

V7X_LANES = 128
V7X_SUBLANES = 8
V7X_MXU_DIM = 256
V7X_VMEM_BYTES = 64 * 1024 * 1024
VMEM_LIMIT_BYTES = V7X_VMEM_BYTES - 6 * 1024 * 1024

ROW_TILE = 512
Q_TILE = 1024
Q_SUB = V7X_MXU_DIM
KEY_CHUNK = V7X_MXU_DIM
CH_TILE = V7X_MXU_DIM
POOL_HALO = V7X_SUBLANES
CONV_ROWS = 256
CONV_HALO = V7X_SUBLANES
FFT_BLOCK = V7X_MXU_DIM
FFT_RADIX = 16
FFT_RES = FFT_RADIX // 2 + 1
FFT_ROWS = 16
SIDE_TICKS = 8
SQRT_HALF = math.sqrt(0.5)
COND_ROWS = 24


def _params(n_axes):
    return pltpu.CompilerParams(dimension_semantics=("arbitrary",) * n_axes,
                                vmem_limit_bytes=VMEM_LIMIT_BYTES)


def _silu(v):
    half = 0.5 * v
    return half + half * jnp.tanh(half)


def _modulate(x, g, shift, scale):
    y = x * lax.rsqrt(jnp.mean(x * x, axis=-1, keepdims=True) + RMS_EPS)
    return (y * g) * (1.0 + scale) + shift


def _nt_dot(a, b):
    return lax.dot_general(a, b, (((1,), (1,)), ((), ())), preferred_element_type=F32)


def _ada_kernel(cond_ref, w_ref, b_ref, o_ref):
    s = _silu(cond_ref[...])
    o_ref[0] = jnp.dot(s, w_ref[0], preferred_element_type=F32, precision=HIGHEST) + b_ref[0]


def _ada(cond, ada_w, ada_b):
    depth, d, d3 = ada_w.shape
    nt = d3 // d
    return pl.pallas_call(
        _ada_kernel,
        out_shape=jax.ShapeDtypeStruct((depth, COND_ROWS, d3), F32),
        grid=(depth, nt),
        in_specs=[pl.BlockSpec((COND_ROWS, d), lambda i, j: (0, 0)),
                  pl.BlockSpec((1, d, d), lambda i, j: (i, 0, j)),
                  pl.BlockSpec((1, 1, d), lambda i, j: (i, 0, j))],
        out_specs=pl.BlockSpec((1, COND_ROWS, d), lambda i, j: (i, 0, j)),
        compiler_params=_params(2), name="ada",
    )(cond, ada_w, ada_b.reshape(depth, 1, d3))


def _rope_slab(t, cos, sin_lo, sin_hi):
    return (t * cos + pltpu.roll(t, V7X_LANES - 16, axis=1) * sin_lo
            + pltpu.roll(t, 16, axis=1) * sin_hi)


def _proj0_kernel(x_ref, ng_ref, sh_ref, sc_ref, cos_ref, slo_ref, shi_ref, w_ref, wvt_ref,
                  q_ref, k_ref, vt_ref, p_ref, g_ref, *, d, q_scale):
    hb = _modulate(x_ref[0], ng_ref[...], sh_ref[0], sc_ref[0]).astype(BF16)
    cos, slo, shi = cos_ref[...], slo_ref[...], shi_ref[...]

    def proj(c0, width):
        return jnp.dot(hb, w_ref[:, c0:c0 + width], preferred_element_type=F32)

    def rope(t, scale):
        for s in range(d // V7X_LANES):
            sl = slice(s * V7X_LANES, (s + 1) * V7X_LANES)
            yield sl, _rope_slab(t[:, sl], cos, slo, shi) * scale

    for sl, r in rope(proj(0, d), q_scale):
        q_ref[0, :, sl] = r.astype(BF16)
    for sl, r in rope(proj(d, d), 1.0):
        k_ref[0, :, sl] = r.astype(BF16)
    vt_ref[0] = _nt_dot(wvt_ref[...], hb).astype(BF16)
    p_ref[0] = proj(3 * d, d)
    g_ref[0] = proj(4 * d, 2 * d).astype(BF16)


def _proj0(x, norm_g, shift, scale, cos, sin_lo, sin_hi, w_bf16, wvt_bf16):
    b, n, d = x.shape
    ncols = w_bf16.shape[1]
    tl = ROW_TILE
    row = lambda j, i: (i, j, 0)
    vec = lambda j, i: (i, 0, 0)
    tab = lambda j, i: (j, 0)
    kern = functools.partial(_proj0_kernel, d=d, q_scale=A_HEAD_DIM ** -0.5 * LOG2E)
    return pl.pallas_call(
        kern,
        out_shape=(jax.ShapeDtypeStruct((b, n, d), BF16), jax.ShapeDtypeStruct((b, n, d), BF16),
                   jax.ShapeDtypeStruct((b, d, n), BF16), jax.ShapeDtypeStruct((b, n, d), F32),
                   jax.ShapeDtypeStruct((b, n, 2 * d), BF16)),
        grid=(n // tl, b),
        in_specs=[pl.BlockSpec((1, tl, d), row),
                  pl.BlockSpec((1, d), lambda j, i: (0, 0)),
                  pl.BlockSpec((1, 1, d), vec), pl.BlockSpec((1, 1, d), vec),
                  pl.BlockSpec((tl, V7X_LANES), tab), pl.BlockSpec((tl, V7X_LANES), tab),
                  pl.BlockSpec((tl, V7X_LANES), tab),
                  pl.BlockSpec((d, ncols), lambda j, i: (0, 0), pipeline_mode=pl.Buffered(1)),
                  pl.BlockSpec((d, d), lambda j, i: (0, 0), pipeline_mode=pl.Buffered(1))],
        out_specs=(pl.BlockSpec((1, tl, d), row), pl.BlockSpec((1, tl, d), row),
                   pl.BlockSpec((1, d, tl), lambda j, i: (i, 0, j)), pl.BlockSpec((1, tl, d), row),
                   pl.BlockSpec((1, tl, 2 * d), row)),
        compiler_params=_params(2), name="proj0",
    )(x, norm_g, shift, scale, cos, sin_lo, sin_hi, w_bf16, wvt_bf16)


def _ctx_kv_kernel(c_ref, ng_ref, sh_ref, sc_ref, wk_ref, wvt_ref, k_ref, vt_ref):
    hb = _modulate(c_ref[...], ng_ref[...], sh_ref[...], sc_ref[...]).astype(BF16)
    k_ref[...] = jnp.dot(hb, wk_ref[...], preferred_element_type=F32).astype(BF16)
    vt_ref[...] = _nt_dot(wvt_ref[...], hb).astype(BF16)


def _ctx_kv(ctx_rows, norm_g, shift_c, scale_c, w_bf16, wvt_bf16):
    rows, d = ctx_rows.shape
    tr = ROW_TILE
    one = lambda i: (0, 0)
    return pl.pallas_call(
        _ctx_kv_kernel,
        out_shape=(jax.ShapeDtypeStruct((rows, d), BF16), jax.ShapeDtypeStruct((d, rows), BF16)),
        grid=(rows // tr,),
        in_specs=[pl.BlockSpec((tr, d), lambda i: (i, 0)),
                  pl.BlockSpec((1, d), one), pl.BlockSpec((1, d), one), pl.BlockSpec((1, d), one),
                  pl.BlockSpec((d, d), lambda i: (0, 1)), pl.BlockSpec((d, d), one)],
        out_specs=(pl.BlockSpec((tr, d), lambda i: (i, 0)), pl.BlockSpec((d, tr), lambda i: (0, i))),
        compiler_params=_params(1), name="ctx_kv",
    )(ctx_rows, norm_g, shift_c, scale_c, w_bf16, wvt_bf16)


def _attn_kernel(q_ref, kc_ref, k_ref, vct_ref, vt_ref, lq1_ref, lk1_ref, lq2_ref, lk2_ref, sg_ref,
                 o_ref, s_ref, *, lam_init):
    n_sub = q_ref.shape[1] // Q_SUB
    n_chunk = (kc_ref.shape[1] + k_ref.shape[1]) // KEY_CHUNK
    n_ctx = kc_ref.shape[1] // KEY_CHUNK
    groups = KEY_CHUNK // V7X_SUBLANES
    lam = (jnp.exp(jnp.sum(lq1_ref[...] * lk1_ref[...], axis=-1, keepdims=True))
           - jnp.exp(jnp.sum(lq2_ref[...] * lk2_ref[...], axis=-1, keepdims=True)) + lam_init)
    lane = lax.broadcasted_iota(jnp.int32, (Q_SUB, A_V_DIM), 1)
    masks = (lane < A_HEAD_DIM, lane >= A_HEAD_DIM)

    def keys(j):
        if j < n_ctx:
            return kc_ref[0, j * KEY_CHUNK:(j + 1) * KEY_CHUNK, :]
        return k_ref[0, (j - n_ctx) * KEY_CHUNK:(j - n_ctx + 1) * KEY_CHUNK, :]

    def values_t(j):
        if j < n_ctx:
            return vct_ref[:, j * KEY_CHUNK:(j + 1) * KEY_CHUNK]
        return vt_ref[0, :, (j - n_ctx) * KEY_CHUNK:(j - n_ctx + 1) * KEY_CHUNK]

    def fold(x, op):
        return op(x.reshape(groups, V7X_SUBLANES, Q_SUB), axis=0)

    state = {}

    def score_task(t, c, j):
        if (t, c, "q") not in state:
            q = q_ref[0, t * Q_SUB:(t + 1) * Q_SUB, :]
            state[t, c, "q"] = jnp.where(masks[c], q, jnp.zeros_like(q))
        s = _nt_dot(keys(j), state[t, c, "q"])
        s_ref[t % 2, c, j * KEY_CHUNK:(j + 1) * KEY_CHUNK, :] = s
        m8 = fold(s, jnp.max)
        state[t, c, "m8"] = m8 if j == 0 else jnp.maximum(state[t, c, "m8"], m8)

    def exp_task(t, c, j):
        if j == 0:
            state[t, c, "m"] = jnp.max(state[t, c, "m8"], axis=0, keepdims=True)
        p = jnp.exp2(s_ref[t % 2, c, j * KEY_CHUNK:(j + 1) * KEY_CHUNK, :] - state[t, c, "m"])
        l8 = fold(p, jnp.sum)
        pv = jnp.dot(values_t(j), p.astype(BF16), preferred_element_type=F32)
        state[t, c, "l8"] = l8 if j == 0 else state[t, c, "l8"] + l8
        state[t, c, "acc"] = pv if j == 0 else state[t, c, "acc"] + pv

    def finish(t):
        l1 = jnp.sum(state[t, 0, "l8"], axis=0, keepdims=True)
        l2 = jnp.sum(state[t, 1, "l8"], axis=0, keepdims=True)
        ot = state[t, 0, "acc"] * (1.0 / l1) - state[t, 1, "acc"] * (lam / l2)
        on = ot * lax.rsqrt(jnp.mean(ot * ot, axis=0, keepdims=True) + SUBLN_EPS)
        o_ref[0, t * Q_SUB:(t + 1) * Q_SUB, :] = ((on.T * sg_ref[...]) * (1.0 - lam_init)).astype(BF16)

    order = [(c, j) for c in range(2) for j in range(n_chunk)]
    for t in range(n_sub + 1):
        for c, j in order:
            if t < n_sub:
                score_task(t, c, j)
            if t > 0:
                exp_task(t - 1, c, j)
        if t > 0:
            finish(t - 1)


def _attention(q, kc, k, vct, vt, lq1, lk1, lq2, lk2, subln_g, lam_init):
    b, n, d = q.shape
    nc = kc.shape[1]
    hd = A_V_DIM
    tq = Q_TILE
    kv = lambda bi, h, i: (bi, 0, h)
    one = lambda bi, h, i: (0, 0)
    kern = functools.partial(_attn_kernel, lam_init=lam_init)
    return pl.pallas_call(
        kern,
        out_shape=jax.ShapeDtypeStruct((b, n, d), BF16),
        grid=(b, A_HEADS, n // tq),
        in_specs=[pl.BlockSpec((1, tq, hd), lambda bi, h, i: (bi, i, h)),
                  pl.BlockSpec((1, nc, hd), kv), pl.BlockSpec((1, n, hd), kv),
                  pl.BlockSpec((hd, nc), lambda bi, h, i: (h, bi)),
                  pl.BlockSpec((1, hd, n), lambda bi, h, i: (bi, h, 0)),
                  pl.BlockSpec((1, A_HEAD_DIM), one), pl.BlockSpec((1, A_HEAD_DIM), one),
                  pl.BlockSpec((1, A_HEAD_DIM), one), pl.BlockSpec((1, A_HEAD_DIM), one),
                  pl.BlockSpec((1, hd), one)],
        out_specs=pl.BlockSpec((1, tq, hd), lambda bi, h, i: (bi, i, h)),
        scratch_shapes=[pltpu.VMEM((2, 2, nc + n, Q_SUB), F32)],
        compiler_params=_params(3), name="diff_attn",
    )(q, kc, k, vct, vt, lq1, lk1, lq2, lk2, subln_g)


def _pool_kernel(p_ref, pw_ref, ps_ref, o_ref, pad_ref, *, n, group):
    rows = ROW_TILE
    zeros = jnp.zeros((POOL_HALO, group), F32)
    for gi, win in enumerate(POOL_WINDOWS):
        sl = slice(gi * group, (gi + 1) * group)
        pad_ref[0:POOL_HALO, :] = zeros
        pad_ref[POOL_HALO + n:2 * POOL_HALO + n, :] = zeros
        pad_ref[POOL_HALO:POOL_HALO + n, :] = p_ref[0, :, sl]
        back = win // 2
        for r0 in range(0, n, rows):
            acc = pad_ref[pl.ds(POOL_HALO + r0 - back, rows), :]
            for j in range(1 - back, win - back):
                acc = acc + pad_ref[pl.ds(POOL_HALO + r0 + j, rows), :]
            t = r0 + lax.broadcasted_iota(jnp.int32, (rows, 1), 0)
            cnt = (jnp.minimum(t + (win - back), n) - jnp.maximum(t - back, 0)).astype(F32)
            m = acc / cnt - pad_ref[pl.ds(POOL_HALO + r0, rows), :]
            y = jnp.dot(m.astype(BF16), pw_ref[gi], preferred_element_type=F32)
            o_ref[0, r0:r0 + rows, sl] = (y * ps_ref[:, sl]).astype(BF16)


def _pool(p, pool_w_bf16, pool_scale):
    b, n, width = p.shape
    ng, group, _ = pool_w_bf16.shape
    kern = functools.partial(_pool_kernel, n=n, group=group)
    return pl.pallas_call(
        kern,
        out_shape=jax.ShapeDtypeStruct((b, n, width), BF16),
        grid=(b,),
        in_specs=[pl.BlockSpec((1, n, width), lambda i: (i, 0, 0)),
                  pl.BlockSpec((ng, group, group), lambda i: (0, 0, 0)),
                  pl.BlockSpec((1, width), lambda i: (0, 0))],
        out_specs=pl.BlockSpec((1, n, width), lambda i: (i, 0, 0)),
        scratch_shapes=[pltpu.VMEM((n + 2 * POOL_HALO, group), F32)],
        compiler_params=_params(1), name="pool",
    )(p, pool_w_bf16, pool_scale)


def _out0_kernel(o_ref, yp_ref, g_ref, x_ref, gate_ref, w_ref, ng_ref, sh_ref, sc_ref,
                 x1_ref, h1_ref, *, aw):
    sg = _silu(g_ref[0].astype(F32))
    a = (o_ref[0].astype(F32) * sg[:, :aw]).astype(BF16)
    c = (yp_ref[0].astype(F32) * sg[:, aw:]).astype(BF16)
    y = (jnp.dot(a, w_ref[0:aw, :], preferred_element_type=F32)
         + jnp.dot(c, w_ref[aw:, :], preferred_element_type=F32))
    x1 = x_ref[0] + gate_ref[0] * y
    x1_ref[0] = x1
    h1_ref[0] = _modulate(x1, ng_ref[...], sh_ref[0], sc_ref[0]).astype(BF16)


def _out0(o, ypool, g, x, gate, w_bf16, norm_g1, shift1, scale1):
    b, n, d = x.shape
    aw = o.shape[2]
    e = g.shape[2]
    tl = ROW_TILE
    row = lambda i, j: (i, j, 0)
    vec = lambda i, j: (i, 0, 0)
    kern = functools.partial(_out0_kernel, aw=aw)
    return pl.pallas_call(
        kern,
        out_shape=(jax.ShapeDtypeStruct((b, n, d), F32), jax.ShapeDtypeStruct((b, n, d), BF16)),
        grid=(b, n // tl),
        in_specs=[pl.BlockSpec((1, tl, aw), row), pl.BlockSpec((1, tl, e - aw), row),
                  pl.BlockSpec((1, tl, e), row), pl.BlockSpec((1, tl, d), row),
                  pl.BlockSpec((1, 1, d), vec),
                  pl.BlockSpec((e, d), lambda i, j: (0, 0)),
                  pl.BlockSpec((1, d), lambda i, j: (0, 0)),
                  pl.BlockSpec((1, 1, d), vec), pl.BlockSpec((1, 1, d), vec)],
        out_specs=(pl.BlockSpec((1, tl, d), row), pl.BlockSpec((1, tl, d), row)),
        compiler_params=_params(2), name="out0",
    )(o, ypool, g, x, gate, w_bf16, norm_g1, shift1, scale1)


def _filter_mlp_kernel(z_ref, w0, b0, f0, w1, b1, f1, w2, b2, f2, o_ref):
    def layer(h, w, b, f):
        return jnp.sin(f[...] * (jnp.dot(h, w[...], preferred_element_type=F32, precision=HIGHEST) + b[...]))
    h = layer(z_ref[...], w0, b0, f0)
    h = layer(h, w1, b1, f1)
    o_ref[...] = layer(h, w2, b2, f2)


def _filter_mlp(z, w0, b0, f0, w1, b1, f1, w2, b2, f2):
    return pl.pallas_call(
        _filter_mlp_kernel,
        out_shape=jax.ShapeDtypeStruct(z.shape, F32),
        compiler_params=pltpu.CompilerParams(vmem_limit_bytes=VMEM_LIMIT_BYTES), name="filter_mlp",
    )(z, w0, b0, f0, w1, b1, f1, w2, b2, f2)


def _aligned(start, multiple):
    return start if isinstance(start, int) else pl.multiple_of(start, multiple)


def _row_chunks(n, body, init=None, unrolled=False):
    if unrolled:
        carry = init
        for c in range(n // CONV_ROWS):
            carry = body(c * CONV_ROWS, carry)
        return carry

    def step(c, carry):
        return body(pl.multiple_of(c * CONV_ROWS, CONV_ROWS), carry)
    return lax.fori_loop(0, n // CONV_ROWS, step, init)


def _dft8_half(x0, x2, x4, x6):
    s04, d04, s26, d26 = x0 + x4, x0 - x4, x2 + x6, x2 - x6
    cd, cs = SQRT_HALF * d26, SQRT_HALF * s26
    re = (s04 + s26, x0 + cd, d04, x0 - cd, s04 - s26)
    im = (None, -(x4 + cs), -d26, x4 - cs, None)
    return re, im


def _fft16_forward(x):
    er, ei = _dft8_half(x[0], x[2], x[4], x[6])
    orr, oi = _dft8_half(x[1], x[3], x[5], x[7])
    tr, ti = [None] * FFT_RES, [None] * FFT_RES
    tr[0] = er[0] + orr[0]
    tr[8] = er[0] - orr[0]
    tr[4], ti[4] = er[4], -orr[4]
    for k in (1, 2, 3):
        wr, wi = math.cos(math.pi * k / 8), -math.sin(math.pi * k / 8)
        pr = wr * orr[k] - wi * oi[k]
        pi = wr * oi[k] + wi * orr[k]
        tr[k], ti[k] = er[k] + pr, ei[k] + pi
        tr[8 - k], ti[8 - k] = er[k] - pr, pi - ei[k]
    return tr, ti


def _fft16_inverse(hr, hi):
    def quad(a0, a4, a1, a2, a3):
        (a1r, a1i), (a2r, a2i), (a3r, a3i) = a1, a2, a3
        lo, hi_ = a0 + a4, a0 - a4
        return (lo + 2.0 * (a1r + a2r + a3r),
                hi_ + 2.0 * (SQRT_HALF * ((a1r - a1i) - (a3r + a3i)) - a2i),
                lo + 2.0 * (a3i - a1i - a2r),
                hi_ + 2.0 * (SQRT_HALF * ((a3r - a3i) - (a1r + a1i)) + a2i))
    even = quad(hr[0] + hr[8], 2.0 * hr[4], *[(hr[k] + hr[8 - k], hi[k] - hi[8 - k]) for k in (1, 2, 3)])
    odd_in = []
    for k in (1, 2, 3):
        dr, di = hr[k] - hr[8 - k], hi[k] + hi[8 - k]
        wr, wi = math.cos(math.pi * k / 8), math.sin(math.pi * k / 8)
        odd_in.append((dr * wr - di * wi, dr * wi + di * wr))
    odd = quad(hr[0] - hr[8], -2.0 * hi[4], *odd_in)
    return [even[0], odd[0], even[1], odd[1], even[2], odd[2], even[3], odd[3]]


def _fft_stage1(u_ref, tw_ref, side_work=None):
    nb = FFT_BLOCK
    per_step = (nb // FFT_ROWS) // SIDE_TICKS

    def chunk(a0):
        x = [u_ref[pl.ds(_aligned(b * nb + a0, FFT_ROWS), FFT_ROWS), :] for b in range(FFT_RADIX // 2)]
        tr, ti = _fft16_forward(x)
        re_rows = pl.ds(a0, FFT_ROWS)
        im_rows = pl.ds(_aligned(nb + a0, FFT_ROWS), FFT_ROWS)
        for k in range(FFT_RES):
            tw_ref[k, re_rows, :] = tr[k].astype(BF16)
            if ti[k] is not None:
                tw_ref[k, im_rows, :] = ti[k].astype(BF16)

    if side_work is not None:
        for i in range(SIDE_TICKS):
            for j in range(per_step):
                chunk((i * per_step + j) * FFT_ROWS)
            side_work(i)
        return

    def body(i, carry):
        chunk(pl.multiple_of(i * FFT_ROWS, FFT_ROWS))
        return carry

    lax.fori_loop(0, nb // FFT_ROWS, body, None)


def _fft_forward_mxu(tw_ref, dfw_ref, k):
    if k in (0, FFT_RES - 1):
        return jnp.dot(dfw_ref[k, :, 0:FFT_BLOCK], tw_ref[k, 0:FFT_BLOCK, :], preferred_element_type=F32)
    return jnp.dot(dfw_ref[k], tw_ref[k], preferred_element_type=F32)


def _fft_stage2_inverse(h_ref, y_ref, side_work=None):
    nb = FFT_BLOCK
    n_chunks = nb // V7X_SUBLANES

    def chunk(a0):
        re_rows = pl.ds(a0, V7X_SUBLANES)
        im_rows = pl.ds(_aligned(nb + a0, V7X_SUBLANES), V7X_SUBLANES)
        hr = [h_ref[k, re_rows, :] for k in range(FFT_RES)]
        hi = [None] + [h_ref[k, im_rows, :] for k in range(1, FFT_RES - 1)] + [None]
        for b, yb in enumerate(_fft16_inverse(hr, hi)):
            y_ref[pl.ds(_aligned(b * nb + a0, V7X_SUBLANES), V7X_SUBLANES), :] = yb

    if side_work is not None:
        for i in range(SIDE_TICKS):
            for j in range(n_chunks // SIDE_TICKS):
                chunk((i * (n_chunks // SIDE_TICKS) + j) * V7X_SUBLANES)
            side_work(i)
        return

    def body(i, carry):
        chunk(pl.multiple_of(i * V7X_SUBLANES, V7X_SUBLANES))
        return carry

    lax.fori_loop(0, n_chunks, body, None)


def _filter_spec_kernel(h_ref, wf_ref, wb_ref, t_ref, dcy_ref, dfw_ref,
                        gr_ref, gi_ref, fwd_ref, bwd_ref, twf_ref, twb_ref, *, n):
    inv_n = 1.0 / (2 * n)
    row = lax.broadcasted_iota(jnp.int32, (CONV_ROWS, CH_TILE), 0)

    def taps(r0, carry):
        rows = pl.ds(r0, CONV_ROWS)
        hm = h_ref[rows, :]
        decay = jnp.exp(-t_ref[rows, :] * dcy_ref[...])
        fwd_ref[rows, :] = jnp.dot(hm, wf_ref[...], preferred_element_type=F32, precision=HIGHEST) * decay
        bwd = jnp.dot(hm, wb_ref[...], preferred_element_type=F32, precision=HIGHEST) * decay
        bwd_ref[rows, :] = jnp.where(row + r0 == 0, 0.0, bwd)
        return carry

    _row_chunks(n, taps)
    _fft_stage1(fwd_ref, twf_ref)
    _fft_stage1(bwd_ref, twb_ref)
    nb = FFT_BLOCK
    for k in range(FFT_RES):
        xf = _fft_forward_mxu(twf_ref, dfw_ref, k)
        xb = _fft_forward_mxu(twb_ref, dfw_ref, k)
        gr_ref[0, k] = (xf[0:nb] + xb[0:nb]) * inv_n
        gi_ref[0, k] = (xf[nb:2 * nb] - xb[nb:2 * nb]) * inv_n


def _filter_spec(hmlp, wout, tcol, dabs, dfw):
    n, kpad = hmlp.shape
    e = dabs.shape[1]
    ct = CH_TILE
    nct = e // ct
    nb = FFT_BLOCK
    const = lambda o, j: (0, 0)
    gshape = jax.ShapeDtypeStruct((H_ORDER, FFT_RES, nb, e), F32)
    gspec = pl.BlockSpec((1, FFT_RES, nb, ct), lambda o, j: (o, 0, 0, j))
    kern = functools.partial(_filter_spec_kernel, n=n)
    return pl.pallas_call(
        kern,
        out_shape=(gshape, gshape),
        grid=(H_ORDER, nct),
        in_specs=[pl.BlockSpec((n, kpad), const),
                  pl.BlockSpec((kpad, ct), lambda o, j: (0, (2 * o) * nct + j)),
                  pl.BlockSpec((kpad, ct), lambda o, j: (0, (2 * o + 1) * nct + j)),
                  pl.BlockSpec((n, 1), const),
                  pl.BlockSpec((1, ct), lambda o, j: (0, j)),
                  pl.BlockSpec((FFT_RES, 2 * nb, 2 * nb), lambda o, j: (0, 0, 0), pipeline_mode=pl.Buffered(1))],
        out_specs=(gspec, gspec),
        scratch_shapes=[pltpu.VMEM((n, ct), F32), pltpu.VMEM((n, ct), F32),
                        pltpu.VMEM((FFT_RES, 2 * nb, ct), BF16), pltpu.VMEM((FFT_RES, 2 * nb, ct), BF16)],
        compiler_params=_params(2), name="filter_spec",
    )(hmlp, wout, wout, tcol, dabs, dfw)


def _hyena_kernel(h_ref, w_ref, cwv_ref, cw1_ref, cw2_ref,
                  cbv_ref, cb1_ref, cb2_ref, dfw_ref, dinv_ref,
                  gr0_ref, gi0_ref, gr1_ref, gi1_ref, fb0_ref, fb1_ref,
                  o_ref, rawa_ref, rawb_ref, u_ref, tw_ref, res_ref, y_ref, *, n):
    halo = CONV_HALO
    nb = FFT_BLOCK
    ct = CH_TILE
    zero_halo = jnp.zeros((halo, 2 * ct), F32)
    for raw_ref in (rawa_ref, rawb_ref):
        raw_ref[0:halo, :] = zero_halo
        raw_ref[halo + n:2 * halo + n, :] = zero_halo

    def project_chunk(raw_ref, pair, r0):
        raw_ref[pl.ds(halo + r0, CONV_ROWS), :] = jnp.dot(
            h_ref[0, pl.ds(r0, CONV_ROWS), :], w_ref[:, 2 * pair * ct:2 * (pair + 1) * ct],
            preferred_element_type=F32)

    def short_conv(raw_ref, half, r0, cw_ref, cb_ref):
        blk = raw_ref[pl.ds(r0, CONV_ROWS + 2 * halo), half * ct:(half + 1) * ct]
        prev = blk[halo - 1:halo - 1 + CONV_ROWS]
        cur = blk[halo:halo + CONV_ROWS]
        nxt = blk[halo + 1:halo + 1 + CONV_ROWS]
        return cb_ref[...] + prev * cw_ref[0:1, :] + cur * cw_ref[1:2, :] + nxt * cw_ref[2:3, :]

    def long_conv(gr_ref, gi_ref, fwd_side=None, inv_side=None):
        _fft_stage1(u_ref, tw_ref, fwd_side)
        for k in range(FFT_RES):
            x = _fft_forward_mxu(tw_ref, dfw_ref, k)
            xr, xi = x[0:nb], x[nb:2 * nb]
            gr, gi = gr_ref[0, k], gi_ref[0, k]
            y = jnp.concatenate([xr * gr - xi * gi, xr * gi + xi * gr], axis=0).astype(BF16)
            if k in (0, FFT_RES - 1):
                res_ref[k, 0:nb, :] = jnp.dot(dinv_ref[k, 0:nb, :], y, preferred_element_type=F32)
            else:
                res_ref[k] = jnp.dot(dinv_ref[k], y, preferred_element_type=F32)
        _fft_stage2_inverse(res_ref, y_ref, inv_side)

    def mixed(raw_ref, half, r0, cw_ref, cb_ref, fb_ref):
        rows = pl.ds(r0, CONV_ROWS)
        return short_conv(raw_ref, half, r0, cw_ref, cb_ref) * (y_ref[rows, :] + u_ref[rows, :] * fb_ref[0])

    def store_u(r0, u):
        u_ref[pl.ds(r0, CONV_ROWS), :] = u

    n_chunks = n // CONV_ROWS
    for c in range(n_chunks + 1):
        if c < n_chunks:
            project_chunk(rawa_ref, 0, c * CONV_ROWS)
        if c > 0:
            store_u((c - 1) * CONV_ROWS, short_conv(rawa_ref, 0, (c - 1) * CONV_ROWS, cwv_ref, cbv_ref))

    pending = [functools.partial(project_chunk, rawb_ref, 1, c * CONV_ROWS) for c in range(n_chunks)]

    def drain(ticks):
        def side(i):
            if i in ticks and pending:
                pending.pop(0)()
        return side

    long_conv(gr0_ref, gi0_ref, fwd_side=drain((3, 7)), inv_side=drain((1, 4, 7)))
    for c in range(n_chunks):
        store_u(c * CONV_ROWS, mixed(rawa_ref, 1, c * CONV_ROWS, cw1_ref, cb1_ref, fb0_ref))
        drain((1, 4, 7))(c)
    assert not pending
    long_conv(gr1_ref, gi1_ref)

    def finish(r0, carry):
        z = mixed(rawb_ref, 0, r0, cw2_ref, cb2_ref, fb1_ref)
        gate = rawb_ref[pl.ds(halo + r0, CONV_ROWS), ct:2 * ct]
        o_ref[0, pl.ds(r0, CONV_ROWS), :] = (z * _silu(gate)).astype(BF16)
        return carry
    _row_chunks(n, finish)


def _hyena(h1, w_tiles_bf16, conv_w, conv_b, dfw, dinv, gr, gi, fbias):
    b, n, d = h1.shape
    e = gr.shape[3]
    ct = CH_TILE
    nct = e // ct
    nb = FFT_BLOCK
    cwspec = lambda k: pl.BlockSpec((SHORT_CONV, ct), lambda j, i, k=k: (0, k * nct + j))
    cbspec = lambda k: pl.BlockSpec((1, ct), lambda j, i, k=k: (0, k * nct + j))
    dspec = pl.BlockSpec((FFT_RES, 2 * nb, 2 * nb), lambda j, i: (0, 0, 0), pipeline_mode=pl.Buffered(1))
    gspec = lambda o: pl.BlockSpec((1, FFT_RES, nb, ct), lambda j, i, o=o: (o, 0, 0, j),
                                   pipeline_mode=pl.Buffered(1))
    vspec = lambda o: pl.BlockSpec((1, 1, ct), lambda j, i, o=o: (o, 0, j))
    kern = functools.partial(_hyena_kernel, n=n)
    return pl.pallas_call(
        kern,
        out_shape=jax.ShapeDtypeStruct((b, n, e), BF16),
        grid=(nct, b),
        in_specs=[pl.BlockSpec((1, n, d), lambda j, i: (i, 0, 0)),
                  pl.BlockSpec((d, 4 * ct), lambda j, i: (0, j)),
                  cwspec(0), cwspec(1), cwspec(2), cbspec(0), cbspec(1), cbspec(2),
                  dspec, dspec,
                  gspec(0), gspec(0), gspec(1), gspec(1), vspec(0), vspec(1)],
        out_specs=pl.BlockSpec((1, n, ct), lambda j, i: (i, 0, j)),
        scratch_shapes=[pltpu.VMEM((n + 2 * CONV_HALO, 2 * ct), F32), pltpu.VMEM((n + 2 * CONV_HALO, 2 * ct), F32),
                        pltpu.VMEM((n, ct), F32),
                        pltpu.VMEM((FFT_RES, 2 * nb, ct), BF16), pltpu.VMEM((FFT_RES, 2 * nb, ct), F32),
                        pltpu.VMEM((n, ct), F32)],
        compiler_params=_params(2), name="hyena_mix",
    )(h1, w_tiles_bf16, conv_w, conv_w, conv_w, conv_b, conv_b, conv_b,
      dfw, dinv, gr, gi, gr, gi, fbias, fbias)


def _out1_kernel(z_ref, x_ref, gate_ref, w_ref, fg_ref, o_ref):
    y = jnp.dot(z_ref[0], w_ref[...], preferred_element_type=F32)
    x2 = x_ref[0] + gate_ref[0] * y
    o_ref[0] = (x2 * lax.rsqrt(jnp.mean(x2 * x2, axis=-1, keepdims=True) + RMS_EPS)) * fg_ref[...]


def _out1(zg, x1, gate, w_bf16, final_g):
    b, n, d = x1.shape
    e = zg.shape[2]
    tl = ROW_TILE
    row = lambda i, j: (i, j, 0)
    return pl.pallas_call(
        _out1_kernel,
        out_shape=jax.ShapeDtypeStruct((b, n, d), F32),
        grid=(b, n // tl),
        in_specs=[pl.BlockSpec((1, tl, e), row), pl.BlockSpec((1, tl, d), row),
                  pl.BlockSpec((1, 1, d), lambda i, j: (i, 0, 0)),
                  pl.BlockSpec((e, d), lambda i, j: (0, 0)),
                  pl.BlockSpec((1, d), lambda i, j: (0, 0))],
        out_specs=pl.BlockSpec((1, tl, d), row),
        compiler_params=_params(2), name="out1",
    )(zg, x1, gate, w_bf16, final_g)


def _rope_tables(n):
    rows = n // GRID_W
    row = jnp.repeat(jnp.arange(rows), GRID_W).astype(F32)
    col = jnp.tile(jnp.arange(GRID_W), rows).astype(F32)
    half = A_HEAD_DIM // 2
    inv = ROPE_THETA ** (-jnp.arange(0, half, 2, dtype=F32) / half)
    ar, ac = row[:, None] * inv, col[:, None] * inv
    ang = jnp.concatenate([ar, ar, ac, ac], axis=-1)
    cos, sin = jnp.cos(ang), jnp.sin(ang)
    first = (jnp.arange(A_HEAD_DIM) % (half)) < (half // 2)
    sin_lo = jnp.where(first, -sin, 0.0)
    sin_hi = jnp.where(first, 0.0, sin)
    rep = V7X_LANES // A_HEAD_DIM
    return jnp.tile(cos, (1, rep)), jnp.tile(sin_lo, (1, rep)), jnp.tile(sin_hi, (1, rep))


def _fft_tables(n):
    nb = FFT_BLOCK
    idx = jnp.arange(nb, dtype=jnp.int32)
    k2 = jnp.arange(FFT_RES, dtype=jnp.int32)
    freq = FFT_RADIX * idx[None, :, None] + k2[:, None, None]
    ang = ((freq * idx[None, None, :]) % (2 * n)).astype(F32) * (math.pi / n)
    c, s = jnp.cos(ang), jnp.sin(ang)
    dfw = jnp.concatenate([jnp.concatenate([c, s], axis=2), jnp.concatenate([-s, c], axis=2)], axis=1)
    return dfw.astype(BF16), jnp.swapaxes(dfw, 1, 2).astype(BF16)


def _filter_features(n, kpad):
    t = jnp.linspace(0.0, 1.0, n, dtype=F32)[:, None]
    w = 2.0 * math.pi * jnp.arange(n, dtype=F32)[:, None] / n
    bands = jnp.linspace(1e-4, FILTER_BANDS - 1, FILTER_BANDS, dtype=F32)[None, :]
    z = jnp.concatenate([t, jnp.cos(bands * w), -jnp.sin(bands * w)], axis=-1)
    return jnp.pad(z, ((0, 0), (0, kpad - z.shape[1]))), t


def _pad2(a, rows, cols):
    return jnp.pad(a, ((0, rows - a.shape[0]), (0, cols - a.shape[1])))


def kernel(x, c, ctx, c_ctx, norm_g, ada_w, ada_b, final_g, a_w_in, a_lam_q1, a_lam_k1, a_lam_q2, a_lam_k2, a_subln_g, a_pool_w, a_pool_scale, a_w_out, h_w_in, h_conv_w, h_conv_b, h_filt_w0, h_filt_b0, h_filt_f0, h_filt_w1, h_filt_b1, h_filt_f1, h_filt_w2, h_filt_b2, h_filt_f2, h_filt_wout, h_filt_bias, h_w_out):
    b, n, d = x.shape
    nc = ctx.shape[1]
    e = h_w_out.shape[1]
    assert norm_g.shape[0] == 2 and a_w_in.shape[0] == 1 and h_w_in.shape[0] == 1
    assert b + 1 <= COND_ROWS and n % ROW_TILE == 0 and n % Q_TILE == 0 and (b * nc) % ROW_TILE == 0
    assert d == A_HEADS * A_V_DIM and e % CH_TILE == 0 and 2 * n == FFT_BLOCK * FFT_RADIX

    cond = jnp.concatenate([c, c_ctx[None, :], jnp.zeros((COND_ROWS - b - 1, d), F32)], axis=0)
    mods = _ada(cond, ada_w, ada_b)
    shift = mods[:, :, 0:d]
    scale = mods[:, :, d:2 * d]
    gate = mods[:, :, 2 * d:3 * d]
    per_batch = lambda m, i: m[i, :b].reshape(b, 1, d)

    lam_init = 0.8 - 0.6 * math.exp(-0.3 * 0)
    w0 = a_w_in[0].astype(BF16)
    cos, sin_lo, sin_hi = _rope_tables(n)
    wvt = a_w_in[0, :, 2 * d:3 * d].T.astype(BF16)
    q, k, vt, p, g = _proj0(x, norm_g[0:1], per_batch(shift, 0), per_batch(scale, 0), cos, sin_lo, sin_hi,
                            w0, wvt)
    kc, vct = _ctx_kv(ctx.reshape(b * nc, d), norm_g[0:1], shift[0, b:b + 1], scale[0, b:b + 1], w0, wvt)
    o = _attention(q, kc.reshape(b, nc, d), k, vct, vt,
                   a_lam_q1, a_lam_k1, a_lam_q2, a_lam_k2, a_subln_g, lam_init)
    ypool = _pool(p, a_pool_w[0].astype(BF16), a_pool_scale)
    x1, h1 = _out0(o, ypool, g, x, per_batch(gate, 0), a_w_out[0].astype(BF16),
                   norm_g[1:2], per_batch(shift, 1), per_batch(scale, 1))

    kpad = V7X_LANES
    z, tcol = _filter_features(n, kpad)
    row1 = lambda a: _pad2(a, 1, kpad)
    hmlp = _filter_mlp(z, _pad2(h_filt_w0[0], kpad, kpad), row1(h_filt_b0), row1(h_filt_f0),
                       _pad2(h_filt_w1[0], kpad, kpad), row1(h_filt_b1), row1(h_filt_f1),
                       _pad2(h_filt_w2[0], kpad, kpad), row1(h_filt_b2), row1(h_filt_f2))
    max_decay = math.log(DECAY_TARGET) / FAST_DECAY_PCT
    min_decay = math.log(DECAY_TARGET) / SLOW_DECAY_PCT
    dabs = jnp.abs(jnp.linspace(min_decay, max_decay, e, dtype=F32))[None, :]
    dfw, dinv = _fft_tables(n)
    wout = _pad2(h_filt_wout[0], kpad, h_filt_wout.shape[2])
    gr, gi = _filter_spec(hmlp, wout, tcol, dabs, dfw)
    nct = e // CH_TILE
    w_tiles = (h_w_in[0].astype(BF16).reshape(d, H_ORDER + 2, nct, CH_TILE)
               .transpose(0, 2, 1, 3).reshape(d, (H_ORDER + 2) * e))
    zg = _hyena(h1, w_tiles, h_conv_w[0], h_conv_b, dfw, dinv, gr, gi,
                h_filt_bias[0].reshape(H_ORDER, 1, e))
    return _out1(zg, x1, per_batch(gate, 1), h_w_out[0].astype(BF16), final_g[None, :])
```

```python
import functools
import math

import jax
import jax.numpy as jnp
from jax import lax
from jax.experimental import pallas as pl
from jax.experimental.pallas import tpu as pltpu

F32 = jnp.float32
BF16 = jnp.bfloat16
HIGHEST = lax.Precision.HIGHEST

GRID_W = 64
A_HEADS = 8
A_HEAD_DIM = 64
A_V_DIM = 2 * A_HEAD_DIM
POOL_WINDOWS = (2, 4, 8, 16)
ROPE_THETA = 10000.0
H_ORDER = 2
SHORT_CONV = 3
FILTER_EMB = 33
FILTER_BANDS = (FILTER_EMB - 1) // 2
DECAY_TARGET = 1e-2
FAST_DECAY_PCT = 0.3
SLOW_DECAY_PCT = 1.5
RMS_EPS = 1e-6
SUBLN_EPS = 1e-5
LOG2E = 1.4426950408889634

V7X_LANES = 128
V7X_SUBLANES = 8
V7X_MXU_DIM = 256
V7X_VMEM_BYTES = 64 * 1024 * 1024
VMEM_LIMIT_BYTES = V7X_VMEM_BYTES - 6 * 1024 * 1024

ROW_TILE = 512
Q_TILE = 1024
Q_SUB = V7X_MXU_DIM
KEY_CHUNK = V7X_MXU_DIM
SCORE_SPAN = 1
CH_TILE = V7X_MXU_DIM
POOL_HALO = V7X_SUBLANES
CONV_ROWS = 256
CONV_HALO = V7X_SUBLANES
FFT_BLOCK = V7X_MXU_DIM
FFT_RADIX = 16
FFT_RES = FFT_RADIX // 2 + 1
FFT_ROWS = 16
SIDE_TICKS = 8
SQRT_HALF = math.sqrt(0.5)
COND_ROWS = 24


def _params(n_axes):
    return pltpu.CompilerParams(dimension_semantics=("arbitrary",) * n_axes,
                                vmem_limit_bytes=VMEM_LIMIT_BYTES)


def _silu(v):
    half = 0.5 * v
    return half + half * jnp.tanh(half)


def _modulate(x, g, shift, scale):
    y = x * lax.rsqrt(jnp.mean(x * x, axis=-1, keepdims=True) + RMS_EPS)
    return (y * g) * (1.0 + scale) + shift


def _nt_dot(a, b):
    return lax.dot_general(a, b, (((1,), (1,)), ((), ())), preferred_element_type=F32)


def _ada_kernel(cond_ref, w_ref, b_ref, o_ref):
    s = _silu(cond_ref[...])
    o_ref[0] = jnp.dot(s, w_ref[0], preferred_element_type=F32, precision=HIGHEST) + b_ref[0]


def _ada(cond, ada_w, ada_b):
    depth, d, d3 = ada_w.shape
    nt = d3 // d
    return pl.pallas_call(
        _ada_kernel,
        out_shape=jax.ShapeDtypeStruct((depth, COND_ROWS, d3), F32),
        grid=(depth, nt),
        in_specs=[pl.BlockSpec((COND_ROWS, d), lambda i, j: (0, 0)),
                  pl.BlockSpec((1, d, d), lambda i, j: (i, 0, j)),
                  pl.BlockSpec((1, 1, d), lambda i, j: (i, 0, j))],
        out_specs=pl.BlockSpec((1, COND_ROWS, d), lambda i, j: (i, 0, j)),
        compiler_params=_params(2), name="ada",
    )(cond, ada_w, ada_b.reshape(depth, 1, d3))


def _rope_slab(t, cos, sin_lo, sin_hi):
    return (t * cos + pltpu.roll(t, V7X_LANES - 16, axis=1) * sin_lo
            + pltpu.roll(t, 16, axis=1) * sin_hi)


def _proj0_kernel(x_ref, ng_ref, sh_ref, sc_ref, cos_ref, slo_ref, shi_ref, w_ref, wvt_ref,
                  q_ref, k_ref, vt_ref, p_ref, g_ref, *, d, q_scale):
    hb = _modulate(x_ref[0], ng_ref[...], sh_ref[0], sc_ref[0]).astype(BF16)
    cos, slo, shi = cos_ref[...], slo_ref[...], shi_ref[...]

    def proj(c0, width):
        return jnp.dot(hb, w_ref[:, c0:c0 + width], preferred_element_type=F32)

    def rope(t, scale):
        for s in range(d // V7X_LANES):
            sl = slice(s * V7X_LANES, (s + 1) * V7X_LANES)
            yield sl, _rope_slab(t[:, sl], cos, slo, shi) * scale

    for sl, r in rope(proj(0, d), q_scale):
        q_ref[0, :, sl] = r.astype(BF16)
    for sl, r in rope(proj(d, d), 1.0):
        k_ref[0, :, sl] = r.astype(BF16)
    vt_ref[0] = _nt_dot(wvt_ref[...], hb).astype(BF16)
    p_ref[0] = proj(3 * d, d)
    g_ref[0] = proj(4 * d, 2 * d).astype(BF16)


def _proj0(x, norm_g, shift, scale, cos, sin_lo, sin_hi, w_bf16, wvt_bf16):
    b, n, d = x.shape
    ncols = w_bf16.shape[1]
    tl = ROW_TILE
    row = lambda j, i: (i, j, 0)
    vec = lambda j, i: (i, 0, 0)
    tab = lambda j, i: (j, 0)
    kern = functools.partial(_proj0_kernel, d=d, q_scale=A_HEAD_DIM ** -0.5 * LOG2E)
    return pl.pallas_call(
        kern,
        out_shape=(jax.ShapeDtypeStruct((b, n, d), BF16), jax.ShapeDtypeStruct((b, n, d), BF16),
                   jax.ShapeDtypeStruct((b, d, n), BF16), jax.ShapeDtypeStruct((b, n, d), F32),
                   jax.ShapeDtypeStruct((b, n, 2 * d), BF16)),
        grid=(n // tl, b),
        in_specs=[pl.BlockSpec((1, tl, d), row),
                  pl.BlockSpec((1, d), lambda j, i: (0, 0)),
                  pl.BlockSpec((1, 1, d), vec), pl.BlockSpec((1, 1, d), vec),
                  pl.BlockSpec((tl, V7X_LANES), tab), pl.BlockSpec((tl, V7X_LANES), tab),
                  pl.BlockSpec((tl, V7X_LANES), tab),
                  pl.BlockSpec((d, ncols), lambda j, i: (0, 0), pipeline_mode=pl.Buffered(1)),
                  pl.BlockSpec((d, d), lambda j, i: (0, 0), pipeline_mode=pl.Buffered(1))],
        out_specs=(pl.BlockSpec((1, tl, d), row), pl.BlockSpec((1, tl, d), row),
                   pl.BlockSpec((1, d, tl), lambda j, i: (i, 0, j)), pl.BlockSpec((1, tl, d), row),
                   pl.BlockSpec((1, tl, 2 * d), row)),
        compiler_params=_params(2), name="proj0",
    )(x, norm_g, shift, scale, cos, sin_lo, sin_hi, w_bf16, wvt_bf16)


def _ctx_kv_kernel(c_ref, ng_ref, sh_ref, sc_ref, wk_ref, wvt_ref, k_ref, vt_ref):
    hb = _modulate(c_ref[...], ng_ref[...], sh_ref[...], sc_ref[...]).astype(BF16)
    k_ref[...] = jnp.dot(hb, wk_ref[...], preferred_element_type=F32).astype(BF16)
    vt_ref[...] = _nt_dot(wvt_ref[...], hb).astype(BF16)


def _ctx_kv(ctx_rows, norm_g, shift_c, scale_c, w_bf16, wvt_bf16):
    rows, d = ctx_rows.shape
    tr = ROW_TILE
    one = lambda i: (0, 0)
    return pl.pallas_call(
        _ctx_kv_kernel,
        out_shape=(jax.ShapeDtypeStruct((rows, d), BF16), jax.ShapeDtypeStruct((d, rows), BF16)),
        grid=(rows // tr,),
        in_specs=[pl.BlockSpec((tr, d), lambda i: (i, 0)),
                  pl.BlockSpec((1, d), one), pl.BlockSpec((1, d), one), pl.BlockSpec((1, d), one),
                  pl.BlockSpec((d, d), lambda i: (0, 1)), pl.BlockSpec((d, d), one)],
        out_specs=(pl.BlockSpec((tr, d), lambda i: (i, 0)), pl.BlockSpec((d, tr), lambda i: (0, i))),
        compiler_params=_params(1), name="ctx_kv",
    )(ctx_rows, norm_g, shift_c, scale_c, w_bf16, wvt_bf16)


def _attn_kernel(q_ref, kc_ref, k_ref, vct_ref, vt_ref, lq1_ref, lk1_ref, lq2_ref, lk2_ref, sg_ref,
                 o_ref, s_ref, *, lam_init):
    n_sub = q_ref.shape[1] // Q_SUB
    n_chunk = (kc_ref.shape[1] + k_ref.shape[1]) // KEY_CHUNK
    n_ctx = kc_ref.shape[1] // KEY_CHUNK
    lam = (jnp.exp(jnp.sum(lq1_ref[...] * lk1_ref[...], axis=-1, keepdims=True))
           - jnp.exp(jnp.sum(lq2_ref[...] * lk2_ref[...], axis=-1, keepdims=True)) + lam_init)
    lane = lax.broadcasted_iota(jnp.int32, (Q_SUB, A_V_DIM), 1)
    masks = (lane < A_HEAD_DIM, lane >= A_HEAD_DIM)

    spans = {0: n_ctx}
    for j0 in range(n_ctx, n_chunk, SCORE_SPAN):
        spans[j0] = min(SCORE_SPAN, n_chunk - j0)

    def keys(j0, count):
        if j0 < n_ctx:
            return kc_ref[0, j0 * KEY_CHUNK:(j0 + count) * KEY_CHUNK, :]
        return k_ref[0, (j0 - n_ctx) * KEY_CHUNK:(j0 - n_ctx + count) * KEY_CHUNK, :]

    def values_t(j):
        if j < n_ctx:
            return vct_ref[:, j * KEY_CHUNK:(j + 1) * KEY_CHUNK]
        return vt_ref[0, :, (j - n_ctx) * KEY_CHUNK:(j - n_ctx + 1) * KEY_CHUNK]

    def fold(x, op):
        return op(x.reshape(x.shape[0] // V7X_SUBLANES, V7X_SUBLANES, Q_SUB), axis=0)

    state = {}

    def score_task(t, c, j0):
        if (t, c, "q") not in state:
            q = q_ref[0, t * Q_SUB:(t + 1) * Q_SUB, :]
            state[t, c, "q"] = jnp.where(masks[c], q, jnp.zeros_like(q))
        s = _nt_dot(keys(j0, spans[j0]), state[t, c, "q"])
        s_ref[t % 2, c, j0 * KEY_CHUNK:(j0 + spans[j0]) * KEY_CHUNK, :] = s
        m8 = fold(s, jnp.max)
        state[t, c, "m8"] = m8 if j0 == 0 else jnp.maximum(state[t, c, "m8"], m8)

    def exp_task(t, c, j):
        if j == 0:
            state[t, c, "m"] = jnp.max(state[t, c, "m8"], axis=0, keepdims=True)
        p = jnp.exp2(s_ref[t % 2, c, j * KEY_CHUNK:(j + 1) * KEY_CHUNK, :] - state[t, c, "m"])
        l8 = fold(p, jnp.sum)
        pv = jnp.dot(values_t(j), p.astype(BF16), preferred_element_type=F32)
        state[t, c, "l8"] = l8 if j == 0 else state[t, c, "l8"] + l8
        state[t, c, "acc"] = pv if j == 0 else state[t, c, "acc"] + pv

    def finish(t):
        l1 = jnp.sum(state[t, 0, "l8"], axis=0, keepdims=True)
        l2 = jnp.sum(state[t, 1, "l8"], axis=0, keepdims=True)
        ot = state[t, 0, "acc"] * (1.0 / l1) - state[t, 1, "acc"] * (lam / l2)
        on = ot * lax.rsqrt(jnp.mean(ot * ot, axis=0, keepdims=True) + SUBLN_EPS)
        o_ref[0, t * Q_SUB:(t + 1) * Q_SUB, :] = ((on.T * sg_ref[...]) * (1.0 - lam_init)).astype(BF16)

    order = [(c, j) for c in range(2) for j in range(n_chunk)]
    for t in range(n_sub + 1):
        for c, j in order:
            if t < n_sub and j in spans:
                score_task(t, c, j)
            if t > 0:
                exp_task(t - 1, c, j)
        if t > 0:
            finish(t - 1)


def _attention(q, kc, k, vct, vt, lq1, lk1, lq2, lk2, subln_g, lam_init):
    b, n, d = q.shape
    nc = kc.shape[1]
    hd = A_V_DIM
    tq = Q_TILE
    kv = lambda bi, h, i: (bi, 0, h)
    one = lambda bi, h, i: (0, 0)
    kern = functools.partial(_attn_kernel, lam_init=lam_init)
    return pl.pallas_call(
        kern,
        out_shape=jax.ShapeDtypeStruct((b, n, d), BF16),
        grid=(b, A_HEADS, n // tq),
        in_specs=[pl.BlockSpec((1, tq, hd), lambda bi, h, i: (bi, i, h)),
                  pl.BlockSpec((1, nc, hd), kv), pl.BlockSpec((1, n, hd), kv),
                  pl.BlockSpec((hd, nc), lambda bi, h, i: (h, bi)),
                  pl.BlockSpec((1, hd, n), lambda bi, h, i: (bi, h, 0)),
                  pl.BlockSpec((1, A_HEAD_DIM), one), pl.BlockSpec((1, A_HEAD_DIM), one),
                  pl.BlockSpec((1, A_HEAD_DIM), one), pl.BlockSpec((1, A_HEAD_DIM), one),
                  pl.BlockSpec((1, hd), one)],
        out_specs=pl.BlockSpec((1, tq, hd), lambda bi, h, i: (bi, i, h)),
        scratch_shapes=[pltpu.VMEM((2, 2, nc + n, Q_SUB), F32)],
        compiler_params=_params(3), name="diff_attn",
    )(q, kc, k, vct, vt, lq1, lk1, lq2, lk2, subln_g)


def _pool_kernel(p_ref, pw_ref, ps_ref, o_ref, pad_ref, *, n, group):
    rows = ROW_TILE
    zeros = jnp.zeros((POOL_HALO, group), F32)
    for gi, win in enumerate(POOL_WINDOWS):
        sl = slice(gi * group, (gi + 1) * group)
        pad_ref[0:POOL_HALO, :] = zeros
        pad_ref[POOL_HALO + n:2 * POOL_HALO + n, :] = zeros
        pad_ref[POOL_HALO:POOL_HALO + n, :] = p_ref[0, :, sl]
        back = win // 2
        for r0 in range(0, n, rows):
            acc = pad_ref[pl.ds(POOL_HALO + r0 - back, rows), :]
            for j in range(1 - back, win - back):
                acc = acc + pad_ref[pl.ds(POOL_HALO + r0 + j, rows), :]
            t = r0 + lax.broadcasted_iota(jnp.int32, (rows, 1), 0)
            cnt = (jnp.minimum(t + (win - back), n) - jnp.maximum(t - back, 0)).astype(F32)
            m = acc / cnt - pad_ref[pl.ds(POOL_HALO + r0, rows), :]
            y = jnp.dot(m.astype(BF16), pw_ref[gi], preferred_element_type=F32)
            o_ref[0, r0:r0 + rows, sl] = (y * ps_ref[:, sl]).astype(BF16)


def _pool(p, pool_w_bf16, pool_scale):
    b, n, width = p.shape
    ng, group, _ = pool_w_bf16.shape
    kern = functools.partial(_pool_kernel, n=n, group=group)
    return pl.pallas_call(
        kern,
        out_shape=jax.ShapeDtypeStruct((b, n, width), BF16),
        grid=(b,),
        in_specs=[pl.BlockSpec((1, n, width), lambda i: (i, 0, 0)),
                  pl.BlockSpec((ng, group, group), lambda i: (0, 0, 0)),
                  pl.BlockSpec((1, width), lambda i: (0, 0))],
        out_specs=pl.BlockSpec((1, n, width), lambda i: (i, 0, 0)),
        scratch_shapes=[pltpu.VMEM((n + 2 * POOL_HALO, group), F32)],
        compiler_params=_params(1), name="pool",
    )(p, pool_w_bf16, pool_scale)


def _out0_kernel(o_ref, yp_ref, g_ref, x_ref, gate_ref, w_ref, ng_ref, sh_ref, sc_ref,
                 x1_ref, h1_ref, *, aw):
    sg = _silu(g_ref[0].astype(F32))
    a = (o_ref[0].astype(F32) * sg[:, :aw]).astype(BF16)
    c = (yp_ref[0].astype(F32) * sg[:, aw:]).astype(BF16)
    y = (jnp.dot(a, w_ref[0:aw, :], preferred_element_type=F32)
         + jnp.dot(c, w_ref[aw:, :], preferred_element_type=F32))
    x1 = x_ref[0] + gate_ref[0] * y
    x1_ref[0] = x1
    h1_ref[0] = _modulate(x1, ng_ref[...], sh_ref[0], sc_ref[0]).astype(BF16)


def _out0(o, ypool, g, x, gate, w_bf16, norm_g1, shift1, scale1):
    b, n, d = x.shape
    aw = o.shape[2]
    e = g.shape[2]
    tl = ROW_TILE
    row = lambda i, j: (i, j, 0)
    vec = lambda i, j: (i, 0, 0)
    kern = functools.partial(_out0_kernel, aw=aw)
    return pl.pallas_call(
        kern,
        out_shape=(jax.ShapeDtypeStruct((b, n, d), F32), jax.ShapeDtypeStruct((b, n, d), BF16)),
        grid=(b, n // tl),
        in_specs=[pl.BlockSpec((1, tl, aw), row), pl.BlockSpec((1, tl, e - aw), row),
                  pl.BlockSpec((1, tl, e), row), pl.BlockSpec((1, tl, d), row),
                  pl.BlockSpec((1, 1, d), vec),
                  pl.BlockSpec((e, d), lambda i, j: (0, 0)),
                  pl.BlockSpec((1, d), lambda i, j: (0, 0)),
                  pl.BlockSpec((1, 1, d), vec), pl.BlockSpec((1, 1, d), vec)],
        out_specs=(pl.BlockSpec((1, tl, d), row), pl.BlockSpec((1, tl, d), row)),
        compiler_params=_params(2), name="out0",
    )(o, ypool, g, x, gate, w_bf16, norm_g1, shift1, scale1)


def _filter_mlp_kernel(z_ref, w0, b0, f0, w1, b1, f1, w2, b2, f2, o_ref):
    def layer(h, w, b, f):
        return jnp.sin(f[...] * (jnp.dot(h, w[...], preferred_element_type=F32, precision=HIGHEST) + b[...]))
    h = layer(z_ref[...], w0, b0, f0)
    h = layer(h, w1, b1, f1)
    o_ref[...] = layer(h, w2, b2, f2)


def _filter_mlp(z, w0, b0, f0, w1, b1, f1, w2, b2, f2):
    return pl.pallas_call(
        _filter_mlp_kernel,
        out_shape=jax.ShapeDtypeStruct(z.shape, F32),
        compiler_params=pltpu.CompilerParams(vmem_limit_bytes=VMEM_LIMIT_BYTES), name="filter_mlp",
    )(z, w0, b0, f0, w1, b1, f1, w2, b2, f2)


def _aligned(start, multiple):
    return start if isinstance(start, int) else pl.multiple_of(start, multiple)


def _row_chunks(n, body, init=None, unrolled=False):
    if unrolled:
        carry = init
        for c in range(n // CONV_ROWS):
            carry = body(c * CONV_ROWS, carry)
        return carry

    def step(c, carry):
        return body(pl.multiple_of(c * CONV_ROWS, CONV_ROWS), carry)
    return lax.fori_loop(0, n // CONV_ROWS, step, init)


def _dft8_half(x0, x2, x4, x6):
    s04, d04, s26, d26 = x0 + x4, x0 - x4, x2 + x6, x2 - x6
    cd, cs = SQRT_HALF * d26, SQRT_HALF * s26
    re = (s04 + s26, x0 + cd, d04, x0 - cd, s04 - s26)
    im = (None, -(x4 + cs), -d26, x4 - cs, None)
    return re, im


def _fft16_forward(x):
    er, ei = _dft8_half(x[0], x[2], x[4], x[6])
    orr, oi = _dft8_half(x[1], x[3], x[5], x[7])
    tr, ti = [None] * FFT_RES, [None] * FFT_RES
    tr[0] = er[0] + orr[0]
    tr[8] = er[0] - orr[0]
    tr[4], ti[4] = er[4], -orr[4]
    for k in (1, 2, 3):
        wr, wi = math.cos(math.pi * k / 8), -math.sin(math.pi * k / 8)
        pr = wr * orr[k] - wi * oi[k]
        pi = wr * oi[k] + wi * orr[k]
        tr[k], ti[k] = er[k] + pr, ei[k] + pi
        tr[8 - k], ti[8 - k] = er[k] - pr, pi - ei[k]
    return tr, ti


def _fft16_inverse(hr, hi):
    def quad(a0, a4, a1, a2, a3):
        (a1r, a1i), (a2r, a2i), (a3r, a3i) = a1, a2, a3
        lo, hi_ = a0 + a4, a0 - a4
        return (lo + 2.0 * (a1r + a2r + a3r),
                hi_ + 2.0 * (SQRT_HALF * ((a1r - a1i) - (a3r + a3i)) - a2i),
                lo + 2.0 * (a3i - a1i - a2r),
                hi_ + 2.0 * (SQRT_HALF * ((a3r - a3i) - (a1r + a1i)) + a2i))
    even = quad(hr[0] + hr[8], 2.0 * hr[4], *[(hr[k] + hr[8 - k], hi[k] - hi[8 - k]) for k in (1, 2, 3)])
    odd_in = []
    for k in (1, 2, 3):
        dr, di = hr[k] - hr[8 - k], hi[k] + hi[8 - k]
        wr, wi = math.cos(math.pi * k / 8), math.sin(math.pi * k / 8)
        odd_in.append((dr * wr - di * wi, dr * wi + di * wr))
    odd = quad(hr[0] - hr[8], -2.0 * hi[4], *odd_in)
    return [even[0], odd[0], even[1], odd[1], even[2], odd[2], even[3], odd[3]]


def _fft_stage1(u_ref, tw_ref, side_work=None):
    nb = FFT_BLOCK
    per_step = (nb // FFT_ROWS) // SIDE_TICKS

    def chunk(a0):
        x = [u_ref[pl.ds(_aligned(b * nb + a0, FFT_ROWS), FFT_ROWS), :] for b in range(FFT_RADIX // 2)]
        tr, ti = _fft16_forward(x)
        re_rows = pl.ds(a0, FFT_ROWS)
        im_rows = pl.ds(_aligned(nb + a0, FFT_ROWS), FFT_ROWS)
        for k in range(FFT_RES):
            tw_ref[k, re_rows, :] = tr[k].astype(BF16)
            if ti[k] is not None:
                tw_ref[k, im_rows, :] = ti[k].astype(BF16)

    if side_work is not None:
        for i in range(SIDE_TICKS):
            for j in range(per_step):
                chunk((i * per_step + j) * FFT_ROWS)
            side_work(i)
        return

    def body(i, carry):
        chunk(pl.multiple_of(i * FFT_ROWS, FFT_ROWS))
        return carry

    lax.fori_loop(0, nb // FFT_ROWS, body, None)


def _fft_forward_mxu(tw_ref, dfw_ref, k):
    if k in (0, FFT_RES - 1):
        return jnp.dot(dfw_ref[k, :, 0:FFT_BLOCK], tw_ref[k, 0:FFT_BLOCK, :], preferred_element_type=F32)
    return jnp.dot(dfw_ref[k], tw_ref[k], preferred_element_type=F32)


def _fft_stage2_inverse(h_ref, y_ref, side_work=None):
    nb = FFT_BLOCK
    n_chunks = nb // V7X_SUBLANES

    def chunk(a0):
        re_rows = pl.ds(a0, V7X_SUBLANES)
        im_rows = pl.ds(_aligned(nb + a0, V7X_SUBLANES), V7X_SUBLANES)
        hr = [h_ref[k, re_rows, :] for k in range(FFT_RES)]
        hi = [None] + [h_ref[k, im_rows, :] for k in range(1, FFT_RES - 1)] + [None]
        for b, yb in enumerate(_fft16_inverse(hr, hi)):
            y_ref[pl.ds(_aligned(b * nb + a0, V7X_SUBLANES), V7X_SUBLANES), :] = yb

    if side_work is not None:
        for i in range(SIDE_TICKS):
            for j in range(n_chunks // SIDE_TICKS):
                chunk((i * (n_chunks // SIDE_TICKS) + j) * V7X_SUBLANES)
            side_work(i)
        return

    def body(i, carry):
        chunk(pl.multiple_of(i * V7X_SUBLANES, V7X_SUBLANES))
        return carry

    lax.fori_loop(0, n_chunks, body, None)


def _filter_spec_kernel(h_ref, wf_ref, wb_ref, t_ref, dcy_ref, dfw_ref,
                        gr_ref, gi_ref, fwd_ref, bwd_ref, twf_ref, twb_ref, *, n):
    inv_n = 1.0 / (2 * n)
    row = lax.broadcasted_iota(jnp.int32, (CONV_ROWS, CH_TILE), 0)

    def taps(r0, carry):
        rows = pl.ds(r0, CONV_ROWS)
        hm = h_ref[rows, :]
        decay = jnp.exp(-t_ref[rows, :] * dcy_ref[...])
        fwd_ref[rows, :] = jnp.dot(hm, wf_ref[...], preferred_element_type=F32, precision=HIGHEST) * decay
        bwd = jnp.dot(hm, wb_ref[...], preferred_element_type=F32, precision=HIGHEST) * decay
        bwd_ref[rows, :] = jnp.where(row + r0 == 0, 0.0, bwd)
        return carry

    _row_chunks(n, taps)
    _fft_stage1(fwd_ref, twf_ref)
    _fft_stage1(bwd_ref, twb_ref)
    nb = FFT_BLOCK
    for k in range(FFT_RES):
        xf = _fft_forward_mxu(twf_ref, dfw_ref, k)
        xb = _fft_forward_mxu(twb_ref, dfw_ref, k)
        gr_ref[0, k] = (xf[0:nb] + xb[0:nb]) * inv_n
        gi_ref[0, k] = (xf[nb:2 * nb] - xb[nb:2 * nb]) * inv_n


def _filter_spec(hmlp, wout, tcol, dabs, dfw):
    n, kpad = hmlp.shape
    e = dabs.shape[1]
    ct = CH_TILE
    nct = e // ct
    nb = FFT_BLOCK
    const = lambda o, j: (0, 0)
    gshape = jax.ShapeDtypeStruct((H_ORDER, FFT_RES, nb, e), F32)
    gspec = pl.BlockSpec((1, FFT_RES, nb, ct), lambda o, j: (o, 0, 0, j))
    kern = functools.partial(_filter_spec_kernel, n=n)
    return pl.pallas_call(
        kern,
        out_shape=(gshape, gshape),
        grid=(H_ORDER, nct),
        in_specs=[pl.BlockSpec((n, kpad), const),
                  pl.BlockSpec((kpad, ct), lambda o, j: (0, (2 * o) * nct + j)),
                  pl.BlockSpec((kpad, ct), lambda o, j: (0, (2 * o + 1) * nct + j)),
                  pl.BlockSpec((n, 1), const),
                  pl.BlockSpec((1, ct), lambda o, j: (0, j)),
                  pl.BlockSpec((FFT_RES, 2 * nb, 2 * nb), lambda o, j: (0, 0, 0), pipeline_mode=pl.Buffered(1))],
        out_specs=(gspec, gspec),
        scratch_shapes=[pltpu.VMEM((n, ct), F32), pltpu.VMEM((n, ct), F32),
                        pltpu.VMEM((FFT_RES, 2 * nb, ct), BF16), pltpu.VMEM((FFT_RES, 2 * nb, ct), BF16)],
        compiler_params=_params(2), name="filter_spec",
    )(hmlp, wout, wout, tcol, dabs, dfw)


def _hyena_kernel(h_ref, w_ref, cwv_ref, cw1_ref, cw2_ref,
                  cbv_ref, cb1_ref, cb2_ref, dfw_ref, dinv_ref,
                  gr0_ref, gi0_ref, gr1_ref, gi1_ref, fb0_ref, fb1_ref,
                  o_ref, rawa_ref, rawb_ref, u_ref, tw_ref, res_ref, y_ref, *, n):
    halo = CONV_HALO
    nb = FFT_BLOCK
    ct = CH_TILE
    zero_halo = jnp.zeros((halo, 2 * ct), F32)
    for raw_ref in (rawa_ref, rawb_ref):
        raw_ref[0:halo, :] = zero_halo
        raw_ref[halo + n:2 * halo + n, :] = zero_halo

    def project_chunk(raw_ref, pair, r0):
        raw_ref[pl.ds(halo + r0, CONV_ROWS), :] = jnp.dot(
            h_ref[0, pl.ds(r0, CONV_ROWS), :], w_ref[:, 2 * pair * ct:2 * (pair + 1) * ct],
            preferred_element_type=F32)

    def short_conv(raw_ref, half, r0, cw_ref, cb_ref):
        blk = raw_ref[pl.ds(r0, CONV_ROWS + 2 * halo), half * ct:(half + 1) * ct]
        prev = blk[halo - 1:halo - 1 + CONV_ROWS]
        cur = blk[halo:halo + CONV_ROWS]
        nxt = blk[halo + 1:halo + 1 + CONV_ROWS]
        return cb_ref[...] + prev * cw_ref[0:1, :] + cur * cw_ref[1:2, :] + nxt * cw_ref[2:3, :]

    def long_conv(gr_ref, gi_ref, fwd_side=None, mxu_side=None, inv_side=None):
        _fft_stage1(u_ref, tw_ref, fwd_side)
        for k in range(FFT_RES):
            x = _fft_forward_mxu(tw_ref, dfw_ref, k)
            xr, xi = x[0:nb], x[nb:2 * nb]
            gr, gi = gr_ref[0, k], gi_ref[0, k]
            y = jnp.concatenate([xr * gr - xi * gi, xr * gi + xi * gr], axis=0).astype(BF16)
            if k in (0, FFT_RES - 1):
                res_ref[k, 0:nb, :] = jnp.dot(dinv_ref[k, 0:nb, :], y, preferred_element_type=F32)
            else:
                res_ref[k] = jnp.dot(dinv_ref[k], y, preferred_element_type=F32)
            if mxu_side is not None:
                mxu_side(k)
        _fft_stage2_inverse(res_ref, y_ref, inv_side)

    def stash(half, r0, val):
        rawa_ref[pl.ds(halo + r0, CONV_ROWS), half * ct:(half + 1) * ct] = val

    def conv_out_times_stashed(half, r0, fb_ref):
        rows = pl.ds(r0, CONV_ROWS)
        mult = rawa_ref[pl.ds(halo + r0, CONV_ROWS), half * ct:(half + 1) * ct]
        return mult * (y_ref[rows, :] + u_ref[rows, :] * fb_ref[0])

    def store_u(r0, u):
        u_ref[pl.ds(r0, CONV_ROWS), :] = u

    n_chunks = n // CONV_ROWS
    for c in range(n_chunks + 1):
        if c < n_chunks:
            project_chunk(rawa_ref, 0, c * CONV_ROWS)
        if c > 0:
            store_u((c - 1) * CONV_ROWS, short_conv(rawa_ref, 0, (c - 1) * CONV_ROWS, cwv_ref, cbv_ref))

    pending = [functools.partial(project_chunk, rawb_ref, 1, c * CONV_ROWS) for c in range(n_chunks)]

    def drain(ticks):
        def side(i):
            if i in ticks and pending:
                pending.pop(0)()
        return side

    def x1_conv(k):
        if k < n_chunks:
            stash(0, k * CONV_ROWS, short_conv(rawa_ref, 1, k * CONV_ROWS, cw1_ref, cb1_ref))

    def x2_conv_gate(k):
        if k < n_chunks:
            gate = rawb_ref[pl.ds(halo + k * CONV_ROWS, CONV_ROWS), ct:2 * ct]
            stash(1, k * CONV_ROWS, short_conv(rawb_ref, 0, k * CONV_ROWS, cw2_ref, cb2_ref) * _silu(gate))

    long_conv(gr0_ref, gi0_ref, fwd_side=drain((3, 7)), mxu_side=x1_conv, inv_side=drain((1, 4, 7)))
    for c in range(n_chunks):
        store_u(c * CONV_ROWS, conv_out_times_stashed(0, c * CONV_ROWS, fb0_ref))
        drain((1, 4, 7))(c)
    assert not pending
    long_conv(gr1_ref, gi1_ref, mxu_side=x2_conv_gate)

    def finish(r0, carry):
        o_ref[0, pl.ds(r0, CONV_ROWS), :] = conv_out_times_stashed(1, r0, fb1_ref).astype(BF16)
        return carry
    _row_chunks(n, finish)


def _hyena(h1, w_tiles_bf16, conv_w, conv_b, dfw, dinv, gr, gi, fbias):
    b, n, d = h1.shape
    e = gr.shape[3]
    ct = CH_TILE
    nct = e // ct
    nb = FFT_BLOCK
    cwspec = lambda k: pl.BlockSpec((SHORT_CONV, ct), lambda j, i, k=k: (0, k * nct + j))
    cbspec = lambda k: pl.BlockSpec((1, ct), lambda j, i, k=k: (0, k * nct + j))
    dspec = pl.BlockSpec((FFT_RES, 2 * nb, 2 * nb), lambda j, i: (0, 0, 0), pipeline_mode=pl.Buffered(1))
    gspec = lambda o: pl.BlockSpec((1, FFT_RES, nb, ct), lambda j, i, o=o: (o, 0, 0, j),
                                   pipeline_mode=pl.Buffered(1))
    vspec = lambda o: pl.BlockSpec((1, 1, ct), lambda j, i, o=o: (o, 0, j))
    kern = functools.partial(_hyena_kernel, n=n)
    return pl.pallas_call(
        kern,
        out_shape=jax.ShapeDtypeStruct((b, n, e), BF16),
        grid=(nct, b),
        in_specs=[pl.BlockSpec((1, n, d), lambda j, i: (i, 0, 0)),
                  pl.BlockSpec((d, 4 * ct), lambda j, i: (0, j)),
                  cwspec(0), cwspec(1), cwspec(2), cbspec(0), cbspec(1), cbspec(2),
                  dspec, dspec,
                  gspec(0), gspec(0), gspec(1), gspec(1), vspec(0), vspec(1)],
        out_specs=pl.BlockSpec((1, n, ct), lambda j, i: (i, 0, j)),
        scratch_shapes=[pltpu.VMEM((n + 2 * CONV_HALO, 2 * ct), F32), pltpu.VMEM((n + 2 * CONV_HALO, 2 * ct), F32),
                        pltpu.VMEM((n, ct), F32),
                        pltpu.VMEM((FFT_RES, 2 * nb, ct), BF16), pltpu.VMEM((FFT_RES, 2 * nb, ct), F32),
                        pltpu.VMEM((n, ct), F32)],
        compiler_params=_params(2), name="hyena_mix",
    )(h1, w_tiles_bf16, conv_w, conv_w, conv_w, conv_b, conv_b, conv_b,
      dfw, dinv, gr, gi, gr, gi, fbias, fbias)


def _out1_kernel(z_ref, x_ref, gate_ref, w_ref, fg_ref, o_ref):
    y = jnp.dot(z_ref[0], w_ref[...], preferred_element_type=F32)
    x2 = x_ref[0] + gate_ref[0] * y
    o_ref[0] = (x2 * lax.rsqrt(jnp.mean(x2 * x2, axis=-1, keepdims=True) + RMS_EPS)) * fg_ref[...]


def _out1(zg, x1, gate, w_bf16, final_g):
    b, n, d = x1.shape
    e = zg.shape[2]
    tl = ROW_TILE
    row = lambda i, j: (i, j, 0)
    return pl.pallas_call(
        _out1_kernel,
        out_shape=jax.ShapeDtypeStruct((b, n, d), F32),
        grid=(b, n // tl),
        in_specs=[pl.BlockSpec((1, tl, e), row), pl.BlockSpec((1, tl, d), row),
                  pl.BlockSpec((1, 1, d), lambda i, j: (i, 0, 0)),
                  pl.BlockSpec((e, d), lambda i, j: (0, 0)),
                  pl.BlockSpec((1, d), lambda i, j: (0, 0))],
        out_specs=pl.BlockSpec((1, tl, d), row),
        compiler_params=_params(2), name="out1",
    )(zg, x1, gate, w_bf16, final_g)


def _rope_tables(n):
    rows = n // GRID_W
    row = jnp.repeat(jnp.arange(rows), GRID_W).astype(F32)
    col = jnp.tile(jnp.arange(GRID_W), rows).astype(F32)
    half = A_HEAD_DIM // 2
    inv = ROPE_THETA ** (-jnp.arange(0, half, 2, dtype=F32) / half)
    ar, ac = row[:, None] * inv, col[:, None] * inv
    ang = jnp.concatenate([ar, ar, ac, ac], axis=-1)
    cos, sin = jnp.cos(ang), jnp.sin(ang)
    first = (jnp.arange(A_HEAD_DIM) % (half)) < (half // 2)
    sin_lo = jnp.where(first, -sin, 0.0)
    sin_hi = jnp.where(first, 0.0, sin)
    rep = V7X_LANES // A_HEAD_DIM
    return jnp.tile(cos, (1, rep)), jnp.tile(sin_lo, (1, rep)), jnp.tile(sin_hi, (1, rep))


def _fft_tables(n):
    nb = FFT_BLOCK
    idx = jnp.arange(nb, dtype=jnp.int32)
    k2 = jnp.arange(FFT_RES, dtype=jnp.int32)
    freq = FFT_RADIX * idx[None, :, None] + k2[:, None, None]
    ang = ((freq * idx[None, None, :]) % (2 * n)).astype(F32) * (math.pi / n)
    c, s = jnp.cos(ang), jnp.sin(ang)
    dfw = jnp.concatenate([jnp.concatenate([c, s], axis=2), jnp.concatenate([-s, c], axis=2)], axis=1)
    return dfw.astype(BF16), jnp.swapaxes(dfw, 1, 2).astype(BF16)


def _filter_features(n, kpad):
    t = jnp.linspace(0.0, 1.0, n, dtype=F32)[:, None]
    w = 2.0 * math.pi * jnp.arange(n, dtype=F32)[:, None] / n
    bands = jnp.linspace(1e-4, FILTER_BANDS - 1, FILTER_BANDS, dtype=F32)[None, :]
    z = jnp.concatenate([t, jnp.cos(bands * w), -jnp.sin(bands * w)], axis=-1)
    return jnp.pad(z, ((0, 0), (0, kpad - z.shape[1]))), t


def _pad2(a, rows, cols):
    return jnp.pad(a, ((0, rows - a.shape[0]), (0, cols - a.shape[1])))


def kernel(x, c, ctx, c_ctx, norm_g, ada_w, ada_b, final_g, a_w_in, a_lam_q1, a_lam_k1, a_lam_q2, a_lam_k2, a_subln_g, a_pool_w, a_pool_scale, a_w_out, h_w_in, h_conv_w, h_conv_b, h_filt_w0, h_filt_b0, h_filt_f0, h_filt_w1, h_filt_b1, h_filt_f1, h_filt_w2, h_filt_b2, h_filt_f2, h_filt_wout, h_filt_bias, h_w_out):
    b, n, d = x.shape
    nc = ctx.shape[1]
    e = h_w_out.shape[1]
    assert norm_g.shape[0] == 2 and a_w_in.shape[0] == 1 and h_w_in.shape[0] == 1
    assert b + 1 <= COND_ROWS and n % ROW_TILE == 0 and n % Q_TILE == 0 and (b * nc) % ROW_TILE == 0
    assert d == A_HEADS * A_V_DIM and e % CH_TILE == 0 and 2 * n == FFT_BLOCK * FFT_RADIX

    cond = jnp.concatenate([c, c_ctx[None, :], jnp.zeros((COND_ROWS - b - 1, d), F32)], axis=0)
    mods = _ada(cond, ada_w, ada_b)
    shift = mods[:, :, 0:d]
    scale = mods[:, :, d:2 * d]
    gate = mods[:, :, 2 * d:3 * d]
    per_batch = lambda m, i: m[i, :b].reshape(b, 1, d)

    lam_init = 0.8 - 0.6 * math.exp(-0.3 * 0)
    w0 = a_w_in[0].astype(BF16)
    cos, sin_lo, sin_hi = _rope_tables(n)
    wvt = a_w_in[0, :, 2 * d:3 * d].T.astype(BF16)
    q, k, vt, p, g = _proj0(x, norm_g[0:1], per_batch(shift, 0), per_batch(scale, 0), cos, sin_lo, sin_hi,
                            w0, wvt)
    kc, vct = _ctx_kv(ctx.reshape(b * nc, d), norm_g[0:1], shift[0, b:b + 1], scale[0, b:b + 1], w0, wvt)
    o = _attention(q, kc.reshape(b, nc, d), k, vct, vt,
                   a_lam_q1, a_lam_k1, a_lam_q2, a_lam_k2, a_subln_g, lam_init)
    ypool = _pool(p, a_pool_w[0].astype(BF16), a_pool_scale)
    x1, h1 = _out0(o, ypool, g, x, per_batch(gate, 0), a_w_out[0].astype(BF16),
                   norm_g[1:2], per_batch(shift, 1), per_batch(scale, 1))

    kpad = V7X_LANES
    z, tcol = _filter_features(n, kpad)
    row1 = lambda a: _pad2(a, 1, kpad)
    hmlp = _filter_mlp(z, _pad2(h_filt_w0[0], kpad, kpad), row1(h_filt_b0), row1(h_filt_f0),
                       _pad2(h_filt_w1[0], kpad, kpad), row1(h_filt_b1), row1(h_filt_f1),
                       _pad2(h_filt_w2[0], kpad, kpad), row1(h_filt_b2), row1(h_filt_f2))
    max_decay = math.log(DECAY_TARGET) / FAST_DECAY_PCT
    min_decay = math.log(DECAY_TARGET) / SLOW_DECAY_PCT
    dabs = jnp.abs(jnp.linspace(min_decay, max_decay, e, dtype=F32))[None, :]
    dfw, dinv = _fft_tables(n)
    wout = _pad2(h_filt_wout[0], kpad, h_filt_wout.shape[2])
    gr, gi = _filter_spec(hmlp, wout, tcol, dabs, dfw)
    nct = e // CH_TILE
    w_tiles = (h_w_in[0].astype(BF16).reshape(d, H_ORDER + 2, nct, CH_TILE)
               .transpose(0, 2, 1, 3).reshape(d, (H_ORDER + 2) * e))
    zg = _hyena(h1, w_tiles, h_conv_w[0], h_conv_b, dfw, dinv, gr, gi,
                h_filt_bias[0].reshape(H_ORDER, 1, e))
    return _out1(zg, x1, per_batch(gate, 1), h_w_out[0].astype(BF16), final_g[None, :])
```

```python
import functools
import math

import jax
import jax.numpy as jnp
from jax import lax
from jax.experimental import pallas as pl
from jax.experimental.pallas import tpu as pltpu

F32 = jnp.float32
BF16 = jnp.bfloat16
HIGHEST = lax.Precision.HIGHEST

GRID_W = 64
A_HEADS = 8
A_HEAD_DIM = 64
A_V_DIM = 2 * A_HEAD_DIM
POOL_WINDOWS = (2, 4, 8, 16)
ROPE_THETA = 10000.0
H_ORDER = 2
SHORT_CONV = 3
FILTER_EMB = 33
FILTER_BANDS = (FILTER_EMB - 1) // 2
DECAY_TARGET = 1e-2
FAST_DECAY_PCT = 0.3
SLOW_DECAY_PCT = 1.5
RMS_EPS = 1e-6
SUBLN_EPS = 1e-5
LOG2E = 1.4426950408889634

V7X_LANES = 128
V7X_SUBLANES = 8
V7X_MXU_DIM = 256
V7X_VMEM_BYTES = 64 * 1024 * 1024
VMEM_LIMIT_BYTES = V7X_VMEM_BYTES - 6 * 1024 * 1024

ROW_TILE = 512
Q_TILE = 2048
Q_SUB = V7X_MXU_DIM
KEY_CHUNK = V7X_MXU_DIM
SCORE_SPAN = 1
CH_TILE = V7X_MXU_DIM
POOL_HALO = V7X_SUBLANES
CONV_ROWS = 256
CONV_HALO = V7X_SUBLANES
FFT_BLOCK = V7X_MXU_DIM
FFT_RADIX = 16
FFT_RES = FFT_RADIX // 2 + 1
FFT_ROWS = 16
SIDE_TICKS = 8
SQRT_HALF = math.sqrt(0.5)
COND_ROWS = 24


def _params(n_axes):
    return pltpu.CompilerParams(dimension_semantics=("arbitrary",) * n_axes,
                                vmem_limit_bytes=VMEM_LIMIT_BYTES)


def _silu(v):
    half = 0.5 * v
    return half + half * jnp.tanh(half)


def _modulate(x, g, shift, scale):
    y = x * lax.rsqrt(jnp.mean(x * x, axis=-1, keepdims=True) + RMS_EPS)
    return (y * g) * (1.0 + scale) + shift


def _nt_dot(a, b):
    return lax.dot_general(a, b, (((1,), (1,)), ((), ())), preferred_element_type=F32)


def _ada_kernel(cond_ref, w_ref, b_ref, o_ref):
    s = _silu(cond_ref[...])
    o_ref[0] = jnp.dot(s, w_ref[0], preferred_element_type=F32, precision=HIGHEST) + b_ref[0]


def _ada(cond, ada_w, ada_b):
    depth, d, d3 = ada_w.shape
    nt = d3 // d
    return pl.pallas_call(
        _ada_kernel,
        out_shape=jax.ShapeDtypeStruct((depth, COND_ROWS, d3), F32),
        grid=(depth, nt),
        in_specs=[pl.BlockSpec((COND_ROWS, d), lambda i, j: (0, 0)),
                  pl.BlockSpec((1, d, d), lambda i, j: (i, 0, j)),
                  pl.BlockSpec((1, 1, d), lambda i, j: (i, 0, j))],
        out_specs=pl.BlockSpec((1, COND_ROWS, d), lambda i, j: (i, 0, j)),
        compiler_params=_params(2), name="ada",
    )(cond, ada_w, ada_b.reshape(depth, 1, d3))


def _rope_slab(t, cos, sin_lo, sin_hi):
    return (t * cos + pltpu.roll(t, V7X_LANES - 16, axis=1) * sin_lo
            + pltpu.roll(t, 16, axis=1) * sin_hi)


def _proj0_kernel(x_ref, ng_ref, sh_ref, sc_ref, cos_ref, slo_ref, shi_ref, w_ref, wvt_ref,
                  q_ref, k_ref, vt_ref, p_ref, g_ref, *, d, q_scale):
    hb = _modulate(x_ref[0], ng_ref[...], sh_ref[0], sc_ref[0]).astype(BF16)
    cos, slo, shi = cos_ref[...], slo_ref[...], shi_ref[...]

    def proj(c0, width):
        return jnp.dot(hb, w_ref[:, c0:c0 + width], preferred_element_type=F32)

    def rope(t, scale):
        for s in range(d // V7X_LANES):
            sl = slice(s * V7X_LANES, (s + 1) * V7X_LANES)
            yield sl, _rope_slab(t[:, sl], cos, slo, shi) * scale

    for sl, r in rope(proj(0, d), q_scale):
        q_ref[0, :, sl] = r.astype(BF16)
    for sl, r in rope(proj(d, d), 1.0):
        k_ref[0, :, sl] = r.astype(BF16)
    vt_ref[0] = _nt_dot(wvt_ref[...], hb).astype(BF16)
    p_ref[0] = proj(3 * d, d)
    g_ref[0] = proj(4 * d, 2 * d).astype(BF16)


def _proj0(x, norm_g, shift, scale, cos, sin_lo, sin_hi, w_bf16, wvt_bf16):
    b, n, d = x.shape
    ncols = w_bf16.shape[1]
    tl = ROW_TILE
    row = lambda j, i: (i, j, 0)
    vec = lambda j, i: (i, 0, 0)
    tab = lambda j, i: (j, 0)
    kern = functools.partial(_proj0_kernel, d=d, q_scale=A_HEAD_DIM ** -0.5 * LOG2E)
    return pl.pallas_call(
        kern,
        out_shape=(jax.ShapeDtypeStruct((b, n, d), BF16), jax.ShapeDtypeStruct((b, n, d), BF16),
                   jax.ShapeDtypeStruct((b, d, n), BF16), jax.ShapeDtypeStruct((b, n, d), F32),
                   jax.ShapeDtypeStruct((b, n, 2 * d), BF16)),
        grid=(n // tl, b),
        in_specs=[pl.BlockSpec((1, tl, d), row),
                  pl.BlockSpec((1, d), lambda j, i: (0, 0)),
                  pl.BlockSpec((1, 1, d), vec), pl.BlockSpec((1, 1, d), vec),
                  pl.BlockSpec((tl, V7X_LANES), tab), pl.BlockSpec((tl, V7X_LANES), tab),
                  pl.BlockSpec((tl, V7X_LANES), tab),
                  pl.BlockSpec((d, ncols), lambda j, i: (0, 0), pipeline_mode=pl.Buffered(1)),
                  pl.BlockSpec((d, d), lambda j, i: (0, 0), pipeline_mode=pl.Buffered(1))],
        out_specs=(pl.BlockSpec((1, tl, d), row), pl.BlockSpec((1, tl, d), row),
                   pl.BlockSpec((1, d, tl), lambda j, i: (i, 0, j)), pl.BlockSpec((1, tl, d), row),
                   pl.BlockSpec((1, tl, 2 * d), row)),
        compiler_params=_params(2), name="proj0",
    )(x, norm_g, shift, scale, cos, sin_lo, sin_hi, w_bf16, wvt_bf16)


def _ctx_kv_kernel(c_ref, ng_ref, sh_ref, sc_ref, wk_ref, wvt_ref, k_ref, vt_ref):
    hb = _modulate(c_ref[...], ng_ref[...], sh_ref[...], sc_ref[...]).astype(BF16)
    k_ref[...] = jnp.dot(hb, wk_ref[...], preferred_element_type=F32).astype(BF16)
    vt_ref[...] = _nt_dot(wvt_ref[...], hb).astype(BF16)


def _ctx_kv(ctx_rows, norm_g, shift_c, scale_c, w_bf16, wvt_bf16):
    rows, d = ctx_rows.shape
    tr = ROW_TILE
    one = lambda i: (0, 0)
    return pl.pallas_call(
        _ctx_kv_kernel,
        out_shape=(jax.ShapeDtypeStruct((rows, d), BF16), jax.ShapeDtypeStruct((d, rows), BF16)),
        grid=(rows // tr,),
        in_specs=[pl.BlockSpec((tr, d), lambda i: (i, 0)),
                  pl.BlockSpec((1, d), one), pl.BlockSpec((1, d), one), pl.BlockSpec((1, d), one),
                  pl.BlockSpec((d, d), lambda i: (0, 1)), pl.BlockSpec((d, d), one)],
        out_specs=(pl.BlockSpec((tr, d), lambda i: (i, 0)), pl.BlockSpec((d, tr), lambda i: (0, i))),
        compiler_params=_params(1), name="ctx_kv",
    )(ctx_rows, norm_g, shift_c, scale_c, w_bf16, wvt_bf16)


def _attn_kernel(q_ref, kc_ref, k_ref, vct_ref, vt_ref, lq1_ref, lk1_ref, lq2_ref, lk2_ref, sg_ref,
                 o_ref, s_ref, *, lam_init):
    n_sub = q_ref.shape[1] // Q_SUB
    n_chunk = (kc_ref.shape[1] + k_ref.shape[1]) // KEY_CHUNK
    n_ctx = kc_ref.shape[1] // KEY_CHUNK
    lam = (jnp.exp(jnp.sum(lq1_ref[...] * lk1_ref[...], axis=-1, keepdims=True))
           - jnp.exp(jnp.sum(lq2_ref[...] * lk2_ref[...], axis=-1, keepdims=True)) + lam_init)
    lane = lax.broadcasted_iota(jnp.int32, (Q_SUB, A_V_DIM), 1)
    masks = (lane < A_HEAD_DIM, lane >= A_HEAD_DIM)

    spans = {0: n_ctx}
    for j0 in range(n_ctx, n_chunk, SCORE_SPAN):
        spans[j0] = min(SCORE_SPAN, n_chunk - j0)

    def keys(j0, count):
        if j0 < n_ctx:
            return kc_ref[0, j0 * KEY_CHUNK:(j0 + count) * KEY_CHUNK, :]
        return k_ref[0, (j0 - n_ctx) * KEY_CHUNK:(j0 - n_ctx + count) * KEY_CHUNK, :]

    def values_t(j):
        if j < n_ctx:
            return vct_ref[:, j * KEY_CHUNK:(j + 1) * KEY_CHUNK]
        return vt_ref[0, :, (j - n_ctx) * KEY_CHUNK:(j - n_ctx + 1) * KEY_CHUNK]

    def fold(x, op):
        return op(x.reshape(x.shape[0] // V7X_SUBLANES, V7X_SUBLANES, Q_SUB), axis=0)

    state = {}

    def score_task(t, c, j0):
        if (t, c, "q") not in state:
            q = q_ref[0, t * Q_SUB:(t + 1) * Q_SUB, :]
            state[t, c, "q"] = jnp.where(masks[c], q, jnp.zeros_like(q))
        s = _nt_dot(keys(j0, spans[j0]), state[t, c, "q"])
        s_ref[t % 2, c, j0 * KEY_CHUNK:(j0 + spans[j0]) * KEY_CHUNK, :] = s
        m8 = fold(s, jnp.max)
        state[t, c, "m8"] = m8 if j0 == 0 else jnp.maximum(state[t, c, "m8"], m8)

    def exp_task(t, c, j):
        if j == 0:
            state[t, c, "m"] = jnp.max(state[t, c, "m8"], axis=0, keepdims=True)
        p = jnp.exp2(s_ref[t % 2, c, j * KEY_CHUNK:(j + 1) * KEY_CHUNK, :] - state[t, c, "m"])
        l8 = fold(p, jnp.sum)
        pv = jnp.dot(values_t(j), p.astype(BF16), preferred_element_type=F32)
        state[t, c, "l8"] = l8 if j == 0 else state[t, c, "l8"] + l8
        state[t, c, "acc"] = pv if j == 0 else state[t, c, "acc"] + pv

    def finish(t):
        l1 = jnp.sum(state[t, 0, "l8"], axis=0, keepdims=True)
        l2 = jnp.sum(state[t, 1, "l8"], axis=0, keepdims=True)
        ot = state[t, 0, "acc"] * (1.0 / l1) - state[t, 1, "acc"] * (lam / l2)
        on = ot * lax.rsqrt(jnp.mean(ot * ot, axis=0, keepdims=True) + SUBLN_EPS)
        o_ref[0, t * Q_SUB:(t + 1) * Q_SUB, :] = ((on.T * sg_ref[...]) * (1.0 - lam_init)).astype(BF16)

    order = [(c, j) for c in range(2) for j in range(n_chunk)]
    for t in range(n_sub + 1):
        for c, j in order:
            if t < n_sub and j in spans:
                score_task(t, c, j)
            if t > 0:
                exp_task(t - 1, c, j)
        if t > 0:
            finish(t - 1)


def _attention(q, kc, k, vct, vt, lq1, lk1, lq2, lk2, subln_g, lam_init):
    b, n, d = q.shape
    nc = kc.shape[1]
    hd = A_V_DIM
    tq = Q_TILE
    kv = lambda bi, h, i: (bi, 0, h)
    one = lambda bi, h, i: (0, 0)
    kern = functools.partial(_attn_kernel, lam_init=lam_init)
    return pl.pallas_call(
        kern,
        out_shape=jax.ShapeDtypeStruct((b, n, d), BF16),
        grid=(b, A_HEADS, n // tq),
        in_specs=[pl.BlockSpec((1, tq, hd), lambda bi, h, i: (bi, i, h)),
                  pl.BlockSpec((1, nc, hd), kv), pl.BlockSpec((1, n, hd), kv),
                  pl.BlockSpec((hd, nc), lambda bi, h, i: (h, bi)),
                  pl.BlockSpec((1, hd, n), lambda bi, h, i: (bi, h, 0)),
                  pl.BlockSpec((1, A_HEAD_DIM), one), pl.BlockSpec((1, A_HEAD_DIM), one),
                  pl.BlockSpec((1, A_HEAD_DIM), one), pl.BlockSpec((1, A_HEAD_DIM), one),
                  pl.BlockSpec((1, hd), one)],
        out_specs=pl.BlockSpec((1, tq, hd), lambda bi, h, i: (bi, i, h)),
        scratch_shapes=[pltpu.VMEM((2, 2, nc + n, Q_SUB), F32)],
        compiler_params=_params(3), name="diff_attn",
    )(q, kc, k, vct, vt, lq1, lk1, lq2, lk2, subln_g)


def _pool_kernel(p_ref, pw_ref, ps_ref, o_ref, pad_ref, *, n, group):
    rows = ROW_TILE
    zeros = jnp.zeros((POOL_HALO, group), F32)
    for gi, win in enumerate(POOL_WINDOWS):
        sl = slice(gi * group, (gi + 1) * group)
        pad_ref[0:POOL_HALO, :] = zeros
        pad_ref[POOL_HALO + n:2 * POOL_HALO + n, :] = zeros
        pad_ref[POOL_HALO:POOL_HALO + n, :] = p_ref[0, :, sl]
        back = win // 2
        for r0 in range(0, n, rows):
            acc = pad_ref[pl.ds(POOL_HALO + r0 - back, rows), :]
            for j in range(1 - back, win - back):
                acc = acc + pad_ref[pl.ds(POOL_HALO + r0 + j, rows), :]
            t = r0 + lax.broadcasted_iota(jnp.int32, (rows, 1), 0)
            cnt = (jnp.minimum(t + (win - back), n) - jnp.maximum(t - back, 0)).astype(F32)
            m = acc / cnt - pad_ref[pl.ds(POOL_HALO + r0, rows), :]
            y = jnp.dot(m.astype(BF16), pw_ref[gi], preferred_element_type=F32)
            o_ref[0, r0:r0 + rows, sl] = (y * ps_ref[:, sl]).astype(BF16)


def _pool(p, pool_w_bf16, pool_scale):
    b, n, width = p.shape
    ng, group, _ = pool_w_bf16.shape
    kern = functools.partial(_pool_kernel, n=n, group=group)
    return pl.pallas_call(
        kern,
        out_shape=jax.ShapeDtypeStruct((b, n, width), BF16),
        grid=(b,),
        in_specs=[pl.BlockSpec((1, n, width), lambda i: (i, 0, 0)),
                  pl.BlockSpec((ng, group, group), lambda i: (0, 0, 0)),
                  pl.BlockSpec((1, width), lambda i: (0, 0))],
        out_specs=pl.BlockSpec((1, n, width), lambda i: (i, 0, 0)),
        scratch_shapes=[pltpu.VMEM((n + 2 * POOL_HALO, group), F32)],
        compiler_params=_params(1), name="pool",
    )(p, pool_w_bf16, pool_scale)


def _out0_kernel(o_ref, yp_ref, g_ref, x_ref, gate_ref, w_ref, ng_ref, sh_ref, sc_ref,
                 x1_ref, h1_ref, *, aw):
    sg = _silu(g_ref[0].astype(F32))
    a = (o_ref[0].astype(F32) * sg[:, :aw]).astype(BF16)
    c = (yp_ref[0].astype(F32) * sg[:, aw:]).astype(BF16)
    y = (jnp.dot(a, w_ref[0:aw, :], preferred_element_type=F32)
         + jnp.dot(c, w_ref[aw:, :], preferred_element_type=F32))
    x1 = x_ref[0] + gate_ref[0] * y
    x1_ref[0] = x1
    h1_ref[0] = _modulate(x1, ng_ref[...], sh_ref[0], sc_ref[0]).astype(BF16)


def _out0(o, ypool, g, x, gate, w_bf16, norm_g1, shift1, scale1):
    b, n, d = x.shape
    aw = o.shape[2]
    e = g.shape[2]
    tl = ROW_TILE
    row = lambda i, j: (i, j, 0)
    vec = lambda i, j: (i, 0, 0)
    kern = functools.partial(_out0_kernel, aw=aw)
    return pl.pallas_call(
        kern,
        out_shape=(jax.ShapeDtypeStruct((b, n, d), F32), jax.ShapeDtypeStruct((b, n, d), BF16)),
        grid=(b, n // tl),
        in_specs=[pl.BlockSpec((1, tl, aw), row), pl.BlockSpec((1, tl, e - aw), row),
                  pl.BlockSpec((1, tl, e), row), pl.BlockSpec((1, tl, d), row),
                  pl.BlockSpec((1, 1, d), vec),
                  pl.BlockSpec((e, d), lambda i, j: (0, 0)),
                  pl.BlockSpec((1, d), lambda i, j: (0, 0)),
                  pl.BlockSpec((1, 1, d), vec), pl.BlockSpec((1, 1, d), vec)],
        out_specs=(pl.BlockSpec((1, tl, d), row), pl.BlockSpec((1, tl, d), row)),
        compiler_params=_params(2), name="out0",
    )(o, ypool, g, x, gate, w_bf16, norm_g1, shift1, scale1)


def _filter_mlp_kernel(z_ref, w0, b0, f0, w1, b1, f1, w2, b2, f2, o_ref):
    def layer(h, w, b, f):
        return jnp.sin(f[...] * (jnp.dot(h, w[...], preferred_element_type=F32, precision=HIGHEST) + b[...]))
    h = layer(z_ref[...], w0, b0, f0)
    h = layer(h, w1, b1, f1)
    o_ref[...] = layer(h, w2, b2, f2)


def _filter_mlp(z, w0, b0, f0, w1, b1, f1, w2, b2, f2):
    return pl.pallas_call(
        _filter_mlp_kernel,
        out_shape=jax.ShapeDtypeStruct(z.shape, F32),
        compiler_params=pltpu.CompilerParams(vmem_limit_bytes=VMEM_LIMIT_BYTES), name="filter_mlp",
    )(z, w0, b0, f0, w1, b1, f1, w2, b2, f2)


def _aligned(start, multiple):
    return start if isinstance(start, int) else pl.multiple_of(start, multiple)


def _row_chunks(n, body, init=None, unrolled=False):
    if unrolled:
        carry = init
        for c in range(n // CONV_ROWS):
            carry = body(c * CONV_ROWS, carry)
        return carry

    def step(c, carry):
        return body(pl.multiple_of(c * CONV_ROWS, CONV_ROWS), carry)
    return lax.fori_loop(0, n // CONV_ROWS, step, init)


def _dft8_half(x0, x2, x4, x6):
    s04, d04, s26, d26 = x0 + x4, x0 - x4, x2 + x6, x2 - x6
    cd, cs = SQRT_HALF * d26, SQRT_HALF * s26
    re = (s04 + s26, x0 + cd, d04, x0 - cd, s04 - s26)
    im = (None, -(x4 + cs), -d26, x4 - cs, None)
    return re, im


def _fft16_forward(x):
    er, ei = _dft8_half(x[0], x[2], x[4], x[6])
    orr, oi = _dft8_half(x[1], x[3], x[5], x[7])
    tr, ti = [None] * FFT_RES, [None] * FFT_RES
    tr[0] = er[0] + orr[0]
    tr[8] = er[0] - orr[0]
    tr[4], ti[4] = er[4], -orr[4]
    for k in (1, 2, 3):
        wr, wi = math.cos(math.pi * k / 8), -math.sin(math.pi * k / 8)
        pr = wr * orr[k] - wi * oi[k]
        pi = wr * oi[k] + wi * orr[k]
        tr[k], ti[k] = er[k] + pr, ei[k] + pi
        tr[8 - k], ti[8 - k] = er[k] - pr, pi - ei[k]
    return tr, ti


def _fft16_inverse(hr, hi):
    def quad(a0, a4, a1, a2, a3):
        (a1r, a1i), (a2r, a2i), (a3r, a3i) = a1, a2, a3
        lo, hi_ = a0 + a4, a0 - a4
        return (lo + 2.0 * (a1r + a2r + a3r),
                hi_ + 2.0 * (SQRT_HALF * ((a1r - a1i) - (a3r + a3i)) - a2i),
                lo + 2.0 * (a3i - a1i - a2r),
                hi_ + 2.0 * (SQRT_HALF * ((a3r - a3i) - (a1r + a1i)) + a2i))
    even = quad(hr[0] + hr[8], 2.0 * hr[4], *[(hr[k] + hr[8 - k], hi[k] - hi[8 - k]) for k in (1, 2, 3)])
    odd_in = []
    for k in (1, 2, 3):
        dr, di = hr[k] - hr[8 - k], hi[k] + hi[8 - k]
        wr, wi = math.cos(math.pi * k / 8), math.sin(math.pi * k / 8)
        odd_in.append((dr * wr - di * wi, dr * wi + di * wr))
    odd = quad(hr[0] - hr[8], -2.0 * hi[4], *odd_in)
    return [even[0], odd[0], even[1], odd[1], even[2], odd[2], even[3], odd[3]]


def _fft_stage1(u_ref, tw_ref, side_work=None):
    nb = FFT_BLOCK
    per_step = (nb // FFT_ROWS) // SIDE_TICKS

    def chunk(a0):
        x = [u_ref[pl.ds(_aligned(b * nb + a0, FFT_ROWS), FFT_ROWS), :] for b in range(FFT_RADIX // 2)]
        tr, ti = _fft16_forward(x)
        re_rows = pl.ds(a0, FFT_ROWS)
        im_rows = pl.ds(_aligned(nb + a0, FFT_ROWS), FFT_ROWS)
        for k in range(FFT_RES):
            tw_ref[k, re_rows, :] = tr[k].astype(BF16)
            if ti[k] is not None:
                tw_ref[k, im_rows, :] = ti[k].astype(BF16)

    if side_work is not None:
        for i in range(SIDE_TICKS):
            for j in range(per_step):
                chunk((i * per_step + j) * FFT_ROWS)
            side_work(i)
        return

    def body(i, carry):
        chunk(pl.multiple_of(i * FFT_ROWS, FFT_ROWS))
        return carry

    lax.fori_loop(0, nb // FFT_ROWS, body, None)


def _fft_forward_mxu(tw_ref, dfw_ref, k):
    if k in (0, FFT_RES - 1):
        return jnp.dot(dfw_ref[k, :, 0:FFT_BLOCK], tw_ref[k, 0:FFT_BLOCK, :], preferred_element_type=F32)
    return jnp.dot(dfw_ref[k], tw_ref[k], preferred_element_type=F32)


def _fft_stage2_inverse(h_ref, y_ref, side_work=None):
    nb = FFT_BLOCK
    n_chunks = nb // V7X_SUBLANES

    def chunk(a0):
        re_rows = pl.ds(a0, V7X_SUBLANES)
        im_rows = pl.ds(_aligned(nb + a0, V7X_SUBLANES), V7X_SUBLANES)
        hr = [h_ref[k, re_rows, :] for k in range(FFT_RES)]
        hi = [None] + [h_ref[k, im_rows, :] for k in range(1, FFT_RES - 1)] + [None]
        for b, yb in enumerate(_fft16_inverse(hr, hi)):
            y_ref[pl.ds(_aligned(b * nb + a0, V7X_SUBLANES), V7X_SUBLANES), :] = yb

    if side_work is not None:
        for i in range(SIDE_TICKS):
            for j in range(n_chunks // SIDE_TICKS):
                chunk((i * (n_chunks // SIDE_TICKS) + j) * V7X_SUBLANES)
            side_work(i)
        return

    def body(i, carry):
        chunk(pl.multiple_of(i * V7X_SUBLANES, V7X_SUBLANES))
        return carry

    lax.fori_loop(0, n_chunks, body, None)


def _filter_spec_kernel(h_ref, wf_ref, wb_ref, t_ref, dcy_ref, dfw_ref,
                        gr_ref, gi_ref, fwd_ref, bwd_ref, twf_ref, twb_ref, *, n):
    inv_n = 1.0 / (2 * n)
    row = lax.broadcasted_iota(jnp.int32, (CONV_ROWS, CH_TILE), 0)

    def taps(r0, carry):
        rows = pl.ds(r0, CONV_ROWS)
        hm = h_ref[rows, :]
        decay = jnp.exp(-t_ref[rows, :] * dcy_ref[...])
        fwd_ref[rows, :] = jnp.dot(hm, wf_ref[...], preferred_element_type=F32, precision=HIGHEST) * decay
        bwd = jnp.dot(hm, wb_ref[...], preferred_element_type=F32, precision=HIGHEST) * decay
        bwd_ref[rows, :] = jnp.where(row + r0 == 0, 0.0, bwd)
        return carry

    _row_chunks(n, taps)
    _fft_stage1(fwd_ref, twf_ref)
    _fft_stage1(bwd_ref, twb_ref)
    nb = FFT_BLOCK
    for k in range(FFT_RES):
        xf = _fft_forward_mxu(twf_ref, dfw_ref, k)
        xb = _fft_forward_mxu(twb_ref, dfw_ref, k)
        gr_ref[0, k] = (xf[0:nb] + xb[0:nb]) * inv_n
        gi_ref[0, k] = (xf[nb:2 * nb] - xb[nb:2 * nb]) * inv_n


def _filter_spec(hmlp, wout, tcol, dabs, dfw):
    n, kpad = hmlp.shape
    e = dabs.shape[1]
    ct = CH_TILE
    nct = e // ct
    nb = FFT_BLOCK
    const = lambda o, j: (0, 0)
    gshape = jax.ShapeDtypeStruct((H_ORDER, FFT_RES, nb, e), F32)
    gspec = pl.BlockSpec((1, FFT_RES, nb, ct), lambda o, j: (o, 0, 0, j))
    kern = functools.partial(_filter_spec_kernel, n=n)
    return pl.pallas_call(
        kern,
        out_shape=(gshape, gshape),
        grid=(H_ORDER, nct),
        in_specs=[pl.BlockSpec((n, kpad), const),
                  pl.BlockSpec((kpad, ct), lambda o, j: (0, (2 * o) * nct + j)),
                  pl.BlockSpec((kpad, ct), lambda o, j: (0, (2 * o + 1) * nct + j)),
                  pl.BlockSpec((n, 1), const),
                  pl.BlockSpec((1, ct), lambda o, j: (0, j)),
                  pl.BlockSpec((FFT_RES, 2 * nb, 2 * nb), lambda o, j: (0, 0, 0), pipeline_mode=pl.Buffered(1))],
        out_specs=(gspec, gspec),
        scratch_shapes=[pltpu.VMEM((n, ct), F32), pltpu.VMEM((n, ct), F32),
                        pltpu.VMEM((FFT_RES, 2 * nb, ct), BF16), pltpu.VMEM((FFT_RES, 2 * nb, ct), BF16)],
        compiler_params=_params(2), name="filter_spec",
    )(hmlp, wout, wout, tcol, dabs, dfw)


def _hyena_kernel(h_ref, w_ref, cwv_ref, cw1_ref, cw2_ref,
                  cbv_ref, cb1_ref, cb2_ref, dfw_ref, dinv_ref,
                  gr0_ref, gi0_ref, gr1_ref, gi1_ref, fb0_ref, fb1_ref,
                  o_ref, rawa_ref, rawb_ref, u_ref, tw_ref, res_ref, y_ref, *, n):
    halo = CONV_HALO
    nb = FFT_BLOCK
    ct = CH_TILE
    zero_halo = jnp.zeros((halo, 2 * ct), F32)
    for raw_ref in (rawa_ref, rawb_ref):
        raw_ref[0:halo, :] = zero_halo
        raw_ref[halo + n:2 * halo + n, :] = zero_halo

    def project_chunk(raw_ref, pair, r0):
        raw_ref[pl.ds(halo + r0, CONV_ROWS), :] = jnp.dot(
            h_ref[0, pl.ds(r0, CONV_ROWS), :], w_ref[:, 2 * pair * ct:2 * (pair + 1) * ct],
            preferred_element_type=F32)

    def short_conv(raw_ref, half, r0, cw_ref, cb_ref):
        blk = raw_ref[pl.ds(r0, CONV_ROWS + 2 * halo), half * ct:(half + 1) * ct]
        prev = blk[halo - 1:halo - 1 + CONV_ROWS]
        cur = blk[halo:halo + CONV_ROWS]
        nxt = blk[halo + 1:halo + 1 + CONV_ROWS]
        return cb_ref[...] + prev * cw_ref[0:1, :] + cur * cw_ref[1:2, :] + nxt * cw_ref[2:3, :]

    def long_conv(gr_ref, gi_ref, fwd_side=None, mxu_side=None, inv_side=None):
        _fft_stage1(u_ref, tw_ref, fwd_side)
        for k in range(FFT_RES):
            x = _fft_forward_mxu(tw_ref, dfw_ref, k)
            xr, xi = x[0:nb], x[nb:2 * nb]
            gr, gi = gr_ref[0, k], gi_ref[0, k]
            y = jnp.concatenate([xr * gr - xi * gi, xr * gi + xi * gr], axis=0).astype(BF16)
            if k in (0, FFT_RES - 1):
                res_ref[k, 0:nb, :] = jnp.dot(dinv_ref[k, 0:nb, :], y, preferred_element_type=F32)
            else:
                res_ref[k] = jnp.dot(dinv_ref[k], y, preferred_element_type=F32)
            if mxu_side is not None:
                mxu_side(k)
        _fft_stage2_inverse(res_ref, y_ref, inv_side)

    def stash(half, r0, val):
        rawa_ref[pl.ds(halo + r0, CONV_ROWS), half * ct:(half + 1) * ct] = val

    def conv_out_times_stashed(half, r0, fb_ref):
        rows = pl.ds(r0, CONV_ROWS)
        mult = rawa_ref[pl.ds(halo + r0, CONV_ROWS), half * ct:(half + 1) * ct]
        return mult * (y_ref[rows, :] + u_ref[rows, :] * fb_ref[0])

    def store_u(r0, u):
        u_ref[pl.ds(r0, CONV_ROWS), :] = u

    n_chunks = n // CONV_ROWS
    for c in range(n_chunks + 1):
        if c < n_chunks:
            project_chunk(rawa_ref, 0, c * CONV_ROWS)
        if c > 0:
            store_u((c - 1) * CONV_ROWS, short_conv(rawa_ref, 0, (c - 1) * CONV_ROWS, cwv_ref, cbv_ref))

    pending = [functools.partial(project_chunk, rawb_ref, 1, c * CONV_ROWS) for c in range(n_chunks)]

    def drain(ticks):
        def side(i):
            if i in ticks and pending:
                pending.pop(0)()
        return side

    def x1_conv(k):
        if k < n_chunks:
            stash(0, k * CONV_ROWS, short_conv(rawa_ref, 1, k * CONV_ROWS, cw1_ref, cb1_ref))

    def x2_conv_gate(k):
        if k < n_chunks:
            gate = rawb_ref[pl.ds(halo + k * CONV_ROWS, CONV_ROWS), ct:2 * ct]
            stash(1, k * CONV_ROWS, short_conv(rawb_ref, 0, k * CONV_ROWS, cw2_ref, cb2_ref) * _silu(gate))

    long_conv(gr0_ref, gi0_ref, fwd_side=drain((3, 7)), mxu_side=x1_conv, inv_side=drain((1, 4, 7)))
    for c in range(n_chunks):
        store_u(c * CONV_ROWS, conv_out_times_stashed(0, c * CONV_ROWS, fb0_ref))
        drain((1, 4, 7))(c)
    assert not pending
    long_conv(gr1_ref, gi1_ref, mxu_side=x2_conv_gate)

    def finish(r0, carry):
        o_ref[0, pl.ds(r0, CONV_ROWS), :] = conv_out_times_stashed(1, r0, fb1_ref).astype(BF16)
        return carry
    _row_chunks(n, finish)


def _hyena(h1, w_tiles_bf16, conv_w, conv_b, dfw, dinv, gr, gi, fbias):
    b, n, d = h1.shape
    e = gr.shape[3]
    ct = CH_TILE
    nct = e // ct
    nb = FFT_BLOCK
    cwspec = lambda k: pl.BlockSpec((SHORT_CONV, ct), lambda j, i, k=k: (0, k * nct + j))
    cbspec = lambda k: pl.BlockSpec((1, ct), lambda j, i, k=k: (0, k * nct + j))
    dspec = pl.BlockSpec((FFT_RES, 2 * nb, 2 * nb), lambda j, i: (0, 0, 0), pipeline_mode=pl.Buffered(1))
    gspec = lambda o: pl.BlockSpec((1, FFT_RES, nb, ct), lambda j, i, o=o: (o, 0, 0, j),
                                   pipeline_mode=pl.Buffered(1))
    vspec = lambda o: pl.BlockSpec((1, 1, ct), lambda j, i, o=o: (o, 0, j))
    kern = functools.partial(_hyena_kernel, n=n)
    return pl.pallas_call(
        kern,
        out_shape=jax.ShapeDtypeStruct((b, n, e), BF16),
        grid=(nct, b),
        in_specs=[pl.BlockSpec((1, n, d), lambda j, i: (i, 0, 0)),
                  pl.BlockSpec((d, 4 * ct), lambda j, i: (0, j)),
                  cwspec(0), cwspec(1), cwspec(2), cbspec(0), cbspec(1), cbspec(2),
                  dspec, dspec,
                  gspec(0), gspec(0), gspec(1), gspec(1), vspec(0), vspec(1)],
        out_specs=pl.BlockSpec((1, n, ct), lambda j, i: (i, 0, j)),
        scratch_shapes=[pltpu.VMEM((n + 2 * CONV_HALO, 2 * ct), F32), pltpu.VMEM((n + 2 * CONV_HALO, 2 * ct), F32),
                        pltpu.VMEM((n, ct), F32),
                        pltpu.VMEM((FFT_RES, 2 * nb, ct), BF16), pltpu.VMEM((FFT_RES, 2 * nb, ct), F32),
                        pltpu.VMEM((n, ct), F32)],
        compiler_params=_params(2), name="hyena_mix",
    )(h1, w_tiles_bf16, conv_w, conv_w, conv_w, conv_b, conv_b, conv_b,
      dfw, dinv, gr, gi, gr, gi, fbias, fbias)


def _out1_kernel(z_ref, x_ref, gate_ref, w_ref, fg_ref, o_ref):
    y = jnp.dot(z_ref[0], w_ref[...], preferred_element_type=F32)
    x2 = x_ref[0] + gate_ref[0] * y
    o_ref[0] = (x2 * lax.rsqrt(jnp.mean(x2 * x2, axis=-1, keepdims=True) + RMS_EPS)) * fg_ref[...]


def _out1(zg, x1, gate, w_bf16, final_g):
    b, n, d = x1.shape
    e = zg.shape[2]
    tl = ROW_TILE
    row = lambda i, j: (i, j, 0)
    return pl.pallas_call(
        _out1_kernel,
        out_shape=jax.ShapeDtypeStruct((b, n, d), F32),
        grid=(b, n // tl),
        in_specs=[pl.BlockSpec((1, tl, e), row), pl.BlockSpec((1, tl, d), row),
                  pl.BlockSpec((1, 1, d), lambda i, j: (i, 0, 0)),
                  pl.BlockSpec((e, d), lambda i, j: (0, 0)),
                  pl.BlockSpec((1, d), lambda i, j: (0, 0))],
        out_specs=pl.BlockSpec((1, tl, d), row),
        compiler_params=_params(2), name="out1",
    )(zg, x1, gate, w_bf16, final_g)


def _rope_tables(n):
    rows = n // GRID_W
    row = jnp.repeat(jnp.arange(rows), GRID_W).astype(F32)
    col = jnp.tile(jnp.arange(GRID_W), rows).astype(F32)
    half = A_HEAD_DIM // 2
    inv = ROPE_THETA ** (-jnp.arange(0, half, 2, dtype=F32) / half)
    ar, ac = row[:, None] * inv, col[:, None] * inv
    ang = jnp.concatenate([ar, ar, ac, ac], axis=-1)
    cos, sin = jnp.cos(ang), jnp.sin(ang)
    first = (jnp.arange(A_HEAD_DIM) % (half)) < (half // 2)
    sin_lo = jnp.where(first, -sin, 0.0)
    sin_hi = jnp.where(first, 0.0, sin)
    rep = V7X_LANES // A_HEAD_DIM
    return jnp.tile(cos, (1, rep)), jnp.tile(sin_lo, (1, rep)), jnp.tile(sin_hi, (1, rep))


def _fft_tables(n):
    nb = FFT_BLOCK
    idx = jnp.arange(nb, dtype=jnp.int32)
    k2 = jnp.arange(FFT_RES, dtype=jnp.int32)
    freq = FFT_RADIX * idx[None, :, None] + k2[:, None, None]
    ang = ((freq * idx[None, None, :]) % (2 * n)).astype(F32) * (math.pi / n)
    c, s = jnp.cos(ang), jnp.sin(ang)
    dfw = jnp.concatenate([jnp.concatenate([c, s], axis=2), jnp.concatenate([-s, c], axis=2)], axis=1)
    return dfw.astype(BF16), jnp.swapaxes(dfw, 1, 2).astype(BF16)


def _filter_features(n, kpad):
    t = jnp.linspace(0.0, 1.0, n, dtype=F32)[:, None]
    w = 2.0 * math.pi * jnp.arange(n, dtype=F32)[:, None] / n
    bands = jnp.linspace(1e-4, FILTER_BANDS - 1, FILTER_BANDS, dtype=F32)[None, :]
    z = jnp.concatenate([t, jnp.cos(bands * w), -jnp.sin(bands * w)], axis=-1)
    return jnp.pad(z, ((0, 0), (0, kpad - z.shape[1]))), t


def _pad2(a, rows, cols):
    return jnp.pad(a, ((0, rows - a.shape[0]), (0, cols - a.shape[1])))


def kernel(x, c, ctx, c_ctx, norm_g, ada_w, ada_b, final_g, a_w_in, a_lam_q1, a_lam_k1, a_lam_q2, a_lam_k2, a_subln_g, a_pool_w, a_pool_scale, a_w_out, h_w_in, h_conv_w, h_conv_b, h_filt_w0, h_filt_b0, h_filt_f0, h_filt_w1, h_filt_b1, h_filt_f1, h_filt_w2, h_filt_b2, h_filt_f2, h_filt_wout, h_filt_bias, h_w_out):
    b, n, d = x.shape
    nc = ctx.shape[1]
    e = h_w_out.shape[1]
    assert norm_g.shape[0] == 2 and a_w_in.shape[0] == 1 and h_w_in.shape[0] == 1
    assert b + 1 <= COND_ROWS and n % ROW_TILE == 0 and n % Q_TILE == 0 and (b * nc) % ROW_TILE == 0
    assert d == A_HEADS * A_V_DIM and e % CH_TILE == 0 and 2 * n == FFT_BLOCK * FFT_RADIX

    cond = jnp.concatenate([c, c_ctx[None, :], jnp.zeros((COND_ROWS - b - 1, d), F32)], axis=0)
    mods = _ada(cond, ada_w, ada_b)
    shift = mods[:, :, 0:d]
    scale = mods[:, :, d:2 * d]
    gate = mods[:, :, 2 * d:3 * d]
    per_batch = lambda m, i: m[i, :b].reshape(b, 1, d)

    lam_init = 0.8 - 0.6 * math.exp(-0.3 * 0)
    w0 = a_w_in[0].astype(BF16)
    cos, sin_lo, sin_hi = _rope_tables(n)
    wvt = a_w_in[0, :, 2 * d:3 * d].T.astype(BF16)
    q, k, vt, p, g = _proj0(x, norm_g[0:1], per_batch(shift, 0), per_batch(scale, 0), cos, sin_lo, sin_hi,
                            w0, wvt)
    kc, vct = _ctx_kv(ctx.reshape(b * nc, d), norm_g[0:1], shift[0, b:b + 1], scale[0, b:b + 1], w0, wvt)
    o = _attention(q, kc.reshape(b, nc, d), k, vct, vt,
                   a_lam_q1, a_lam_k1, a_lam_q2, a_lam_k2, a_subln_g, lam_init)
    ypool = _pool(p, a_pool_w[0].astype(BF16), a_pool_scale)
    x1, h1 = _out0(o, ypool, g, x, per_batch(gate, 0), a_w_out[0].astype(BF16),
                   norm_g[1:2], per_batch(shift, 1), per_batch(scale, 1))

    kpad = V7X_LANES
    z, tcol = _filter_features(n, kpad)
    row1 = lambda a: _pad2(a, 1, kpad)
    hmlp = _filter_mlp(z, _pad2(h_filt_w0[0], kpad, kpad), row1(h_filt_b0), row1(h_filt_f0),
                       _pad2(h_filt_w1[0], kpad, kpad), row1(h_filt_b1), row1(h_filt_f1),
                       _pad2(h_filt_w2[0], kpad, kpad), row1(h_filt_b2), row1(h_filt_f2))
    max_decay = math.log(DECAY_TARGET) / FAST_DECAY_PCT
    min_decay = math.log(DECAY_TARGET) / SLOW_DECAY_PCT
    dabs = jnp.abs(jnp.linspace(min_decay, max_decay, e, dtype=F32))[None, :]
    dfw, dinv = _fft_tables(n)
    wout = _pad2(h_filt_wout[0], kpad, h_filt_wout.shape[2])
    gr, gi = _filter_spec(hmlp, wout, tcol, dabs, dfw)
    nct = e // CH_TILE
    w_tiles = (h_w_in[0].astype(BF16).reshape(d, H_ORDER + 2, nct, CH_TILE)
               .transpose(0, 2, 1, 3).reshape(d, (H_ORDER + 2) * e))
    zg = _hyena(h1, w_tiles, h_conv_w[0], h_conv_b, dfw, dinv, gr, gi,
                h_filt_bias[0].reshape(H_ORDER, 1, e))
    return _out1(zg, x1, per_batch(gate, 1), h_w_out[0].astype(BF16), final_g[None, :])
```

```python
import functools
import math

import jax
import jax.numpy as jnp
from jax import lax
from jax.experimental import pallas as pl
from jax.experimental.pallas import tpu as pltpu

F32 = jnp.float32
BF16 = jnp.bfloat16
HIGHEST = lax.Precision.HIGHEST

GRID_W = 64
A_HEADS = 8
A_HEAD_DIM = 64
A_V_DIM = 2 * A_HEAD_DIM
POOL_WINDOWS = (2, 4, 8, 16)
ROPE_THETA = 10000.0
H_ORDER = 2
SHORT_CONV = 3
FILTER_EMB = 33
FILTER_BANDS = (FILTER_EMB - 1) // 2
DECAY_TARGET = 1e-2
FAST_DECAY_PCT = 0.3
SLOW_DECAY_PCT = 1.5
RMS_EPS = 1e-6
SUBLN_EPS = 1e-5
LOG2E = 1.4426950408889634

V7X_LANES = 128
V7X_SUBLANES = 8
V7X_MXU_DIM = 256
V7X_VMEM_BYTES = 64 * 1024 * 1024
VMEM_LIMIT_BYTES = V7X_VMEM_BYTES - 6 * 1024 * 1024

ROW_TILE = 512
Q_TILE = 2048
Q_SUB = V7X_MXU_DIM
KEY_CHUNK = V7X_MXU_DIM
CH_TILE = V7X_MXU_DIM
POOL_HALO = V7X_SUBLANES
CONV_ROWS = 256
CONV_HALO = V7X_SUBLANES
FFT_BLOCK = V7X_MXU_DIM
FFT_RADIX = 16
FFT_RES = FFT_RADIX // 2 + 1
FFT_ROWS = 16
SIDE_TICKS = 8
SQRT_HALF = math.sqrt(0.5)
COND_ROWS = 24


def _params(n_axes):
    return pltpu.CompilerParams(dimension_semantics=("arbitrary",) * n_axes,
                                vmem_limit_bytes=VMEM_LIMIT_BYTES)


def _silu(v):
    half = 0.5 * v
    return half + half * jnp.tanh(half)


def _modulate(x, g, shift, scale):
    y = x * lax.rsqrt(jnp.mean(x * x, axis=-1, keepdims=True) + RMS_EPS)
    return (y * g) * (1.0 + scale) + shift


def _nt_dot(a, b):
    return lax.dot_general(a, b, (((1,), (1,)), ((), ())), preferred_element_type=F32)


def _ada_kernel(cond_ref, w_ref, b_ref, o_ref):
    s = _silu(cond_ref[...])
    o_ref[0] = jnp.dot(s, w_ref[0], preferred_element_type=F32, precision=HIGHEST) + b_ref[0]


def _ada(cond, ada_w, ada_b):
    depth, d, d3 = ada_w.shape
    nt = d3 // d
    return pl.pallas_call(
        _ada_kernel,
        out_shape=jax.ShapeDtypeStruct((depth, COND_ROWS, d3), F32),
        grid=(depth, nt),
        in_specs=[pl.BlockSpec((COND_ROWS, d), lambda i, j: (0, 0)),
                  pl.BlockSpec((1, d, d), lambda i, j: (i, 0, j)),
                  pl.BlockSpec((1, 1, d), lambda i, j: (i, 0, j))],
        out_specs=pl.BlockSpec((1, COND_ROWS, d), lambda i, j: (i, 0, j)),
        compiler_params=_params(2), name="ada",
    )(cond, ada_w, ada_b.reshape(depth, 1, d3))


def _rope_slab(t, cos, sin_lo, sin_hi):
    return (t * cos + pltpu.roll(t, V7X_LANES - 16, axis=1) * sin_lo
            + pltpu.roll(t, 16, axis=1) * sin_hi)


def _proj0_kernel(x_ref, ng_ref, sh_ref, sc_ref, cos_ref, slo_ref, shi_ref, w_ref, wvt_ref,
                  q_ref, k_ref, vt_ref, p_ref, g_ref, *, d, q_scale):
    hb = _modulate(x_ref[0], ng_ref[...], sh_ref[0], sc_ref[0]).astype(BF16)
    cos, slo, shi = cos_ref[...], slo_ref[...], shi_ref[...]

    def proj(c0, width):
        return jnp.dot(hb, w_ref[:, c0:c0 + width], preferred_element_type=F32)

    def rope(t, scale):
        for s in range(d // V7X_LANES):
            sl = slice(s * V7X_LANES, (s + 1) * V7X_LANES)
            yield sl, _rope_slab(t[:, sl], cos, slo, shi) * scale

    for sl, r in rope(proj(0, d), q_scale):
        q_ref[0, :, sl] = r.astype(BF16)
    for sl, r in rope(proj(d, d), 1.0):
        k_ref[0, :, sl] = r.astype(BF16)
    vt_ref[0] = _nt_dot(wvt_ref[...], hb).astype(BF16)
    p_ref[0] = proj(3 * d, d)
    g_ref[0] = proj(4 * d, 2 * d).astype(BF16)


def _proj0(x, norm_g, shift, scale, cos, sin_lo, sin_hi, w_bf16, wvt_bf16):
    b, n, d = x.shape
    ncols = w_bf16.shape[1]
    tl = ROW_TILE
    row = lambda j, i: (i, j, 0)
    vec = lambda j, i: (i, 0, 0)
    tab = lambda j, i: (j, 0)
    kern = functools.partial(_proj0_kernel, d=d, q_scale=A_HEAD_DIM ** -0.5 * LOG2E)
    return pl.pallas_call(
        kern,
        out_shape=(jax.ShapeDtypeStruct((b, n, d), BF16), jax.ShapeDtypeStruct((b, n, d), BF16),
                   jax.ShapeDtypeStruct((b, d, n), BF16), jax.ShapeDtypeStruct((b, n, d), F32),
                   jax.ShapeDtypeStruct((b, n, 2 * d), BF16)),
        grid=(n // tl, b),
        in_specs=[pl.BlockSpec((1, tl, d), row),
                  pl.BlockSpec((1, d), lambda j, i: (0, 0)),
                  pl.BlockSpec((1, 1, d), vec), pl.BlockSpec((1, 1, d), vec),
                  pl.BlockSpec((tl, V7X_LANES), tab), pl.BlockSpec((tl, V7X_LANES), tab),
                  pl.BlockSpec((tl, V7X_LANES), tab),
                  pl.BlockSpec((d, ncols), lambda j, i: (0, 0), pipeline_mode=pl.Buffered(1)),
                  pl.BlockSpec((d, d), lambda j, i: (0, 0), pipeline_mode=pl.Buffered(1))],
        out_specs=(pl.BlockSpec((1, tl, d), row), pl.BlockSpec((1, tl, d), row),
                   pl.BlockSpec((1, d, tl), lambda j, i: (i, 0, j)), pl.BlockSpec((1, tl, d), row),
                   pl.BlockSpec((1, tl, 2 * d), row)),
        compiler_params=_params(2), name="proj0",
    )(x, norm_g, shift, scale, cos, sin_lo, sin_hi, w_bf16, wvt_bf16)


def _ctx_kv_kernel(c_ref, ng_ref, sh_ref, sc_ref, wk_ref, wvt_ref, k_ref, vt_ref):
    hb = _modulate(c_ref[...], ng_ref[...], sh_ref[...], sc_ref[...]).astype(BF16)
    k_ref[...] = jnp.dot(hb, wk_ref[...], preferred_element_type=F32).astype(BF16)
    vt_ref[...] = _nt_dot(wvt_ref[...], hb).astype(BF16)


def _ctx_kv(ctx_rows, norm_g, shift_c, scale_c, w_bf16, wvt_bf16):
    rows, d = ctx_rows.shape
    tr = ROW_TILE
    one = lambda i: (0, 0)
    return pl.pallas_call(
        _ctx_kv_kernel,
        out_shape=(jax.ShapeDtypeStruct((rows, d), BF16), jax.ShapeDtypeStruct((d, rows), BF16)),
        grid=(rows // tr,),
        in_specs=[pl.BlockSpec((tr, d), lambda i: (i, 0)),
                  pl.BlockSpec((1, d), one), pl.BlockSpec((1, d), one), pl.BlockSpec((1, d), one),
                  pl.BlockSpec((d, d), lambda i: (0, 1)), pl.BlockSpec((d, d), one)],
        out_specs=(pl.BlockSpec((tr, d), lambda i: (i, 0)), pl.BlockSpec((d, tr), lambda i: (0, i))),
        compiler_params=_params(1), name="ctx_kv",
    )(ctx_rows, norm_g, shift_c, scale_c, w_bf16, wvt_bf16)


def _attn_kernel(q_ref, kc_ref, k_ref, vct_ref, vt_ref, lq1_ref, lk1_ref, lq2_ref, lk2_ref, sg_ref,
                 o_ref, s_ref, *, lam_init):
    n_sub = q_ref.shape[1] // Q_SUB
    n_chunk = (kc_ref.shape[1] + k_ref.shape[1]) // KEY_CHUNK
    n_ctx = kc_ref.shape[1] // KEY_CHUNK
    lam = (jnp.exp(jnp.sum(lq1_ref[...] * lk1_ref[...], axis=-1, keepdims=True))
           - jnp.exp(jnp.sum(lq2_ref[...] * lk2_ref[...], axis=-1, keepdims=True)) + lam_init)
    lane = lax.broadcasted_iota(jnp.int32, (Q_SUB, A_V_DIM), 1)
    masks = (lane < A_HEAD_DIM, lane >= A_HEAD_DIM)

    def keys(j):
        if j < n_ctx:
            return kc_ref[0, j * KEY_CHUNK:(j + 1) * KEY_CHUNK, :]
        return k_ref[0, (j - n_ctx) * KEY_CHUNK:(j - n_ctx + 1) * KEY_CHUNK, :]

    def values_t(j):
        if j < n_ctx:
            return vct_ref[:, j * KEY_CHUNK:(j + 1) * KEY_CHUNK]
        return vt_ref[0, :, (j - n_ctx) * KEY_CHUNK:(j - n_ctx + 1) * KEY_CHUNK]

    def fold(x, op):
        return op(x.reshape(x.shape[0] // V7X_SUBLANES, V7X_SUBLANES, Q_SUB), axis=0)

    state = {}

    def score_task(t, c, j):
        if (t, c, "q") not in state:
            q = q_ref[0, t * Q_SUB:(t + 1) * Q_SUB, :]
            state[t, c, "q"] = jnp.where(masks[c], q, jnp.zeros_like(q))
        s = _nt_dot(keys(j), state[t, c, "q"])
        s_ref[t % 2, c, j * KEY_CHUNK:(j + 1) * KEY_CHUNK, :] = s
        m8 = fold(s, jnp.max)
        state[t, c, "m8"] = m8 if j == 0 else jnp.maximum(state[t, c, "m8"], m8)

    def exp_task(t, c, j):
        if j == 0:
            state[t, c, "m"] = jnp.max(state[t, c, "m8"], axis=0, keepdims=True)
        p = jnp.exp2(s_ref[t % 2, c, j * KEY_CHUNK:(j + 1) * KEY_CHUNK, :] - state[t, c, "m"])
        l8 = fold(p, jnp.sum)
        pv = jnp.dot(values_t(j), p.astype(BF16), preferred_element_type=F32)
        state[t, c, "l8"] = l8 if j == 0 else state[t, c, "l8"] + l8
        state[t, c, "acc"] = pv if j == 0 else state[t, c, "acc"] + pv

    def finish(t):
        l1 = jnp.sum(state[t, 0, "l8"], axis=0, keepdims=True)
        l2 = jnp.sum(state[t, 1, "l8"], axis=0, keepdims=True)
        ot = state[t, 0, "acc"] * (1.0 / l1) - state[t, 1, "acc"] * (lam / l2)
        on = ot * lax.rsqrt(jnp.mean(ot * ot, axis=0, keepdims=True) + SUBLN_EPS)
        o_ref[0, t * Q_SUB:(t + 1) * Q_SUB, :] = ((on.T * sg_ref[...]) * (1.0 - lam_init)).astype(BF16)

    order = [(c, j) for c in range(2) for j in range(n_chunk)]
    for t in range(n_sub + 1):
        for c, j in order:
            if t < n_sub:
                score_task(t, c, j)
            if t > 0:
                exp_task(t - 1, c, j)
        if t > 0:
            finish(t - 1)


def _attention(q, kc, k, vct, vt, lq1, lk1, lq2, lk2, subln_g, lam_init):
    b, n, d = q.shape
    nc = kc.shape[1]
    hd = A_V_DIM
    tq = Q_TILE
    kv = lambda bi, h, i: (bi, 0, h)
    one = lambda bi, h, i: (0, 0)
    kern = functools.partial(_attn_kernel, lam_init=lam_init)
    return pl.pallas_call(
        kern,
        out_shape=jax.ShapeDtypeStruct((b, n, d), BF16),
        grid=(b, A_HEADS, n // tq),
        in_specs=[pl.BlockSpec((1, tq, hd), lambda bi, h, i: (bi, i, h)),
                  pl.BlockSpec((1, nc, hd), kv), pl.BlockSpec((1, n, hd), kv),
                  pl.BlockSpec((hd, nc), lambda bi, h, i: (h, bi)),
                  pl.BlockSpec((1, hd, n), lambda bi, h, i: (bi, h, 0)),
                  pl.BlockSpec((1, A_HEAD_DIM), one), pl.BlockSpec((1, A_HEAD_DIM), one),
                  pl.BlockSpec((1, A_HEAD_DIM), one), pl.BlockSpec((1, A_HEAD_DIM), one),
                  pl.BlockSpec((1, hd), one)],
        out_specs=pl.BlockSpec((1, tq, hd), lambda bi, h, i: (bi, i, h)),
        scratch_shapes=[pltpu.VMEM((2, 2, nc + n, Q_SUB), F32)],
        compiler_params=_params(3), name="diff_attn",
    )(q, kc, k, vct, vt, lq1, lk1, lq2, lk2, subln_g)


def _pool_kernel(p_ref, pw_ref, ps_ref, o_ref, pad_ref, *, n, group):
    rows = ROW_TILE
    zeros = jnp.zeros((POOL_HALO, group), F32)
    for gi, win in enumerate(POOL_WINDOWS):
        sl = slice(gi * group, (gi + 1) * group)
        pad_ref[0:POOL_HALO, :] = zeros
        pad_ref[POOL_HALO + n:2 * POOL_HALO + n, :] = zeros
        pad_ref[POOL_HALO:POOL_HALO + n, :] = p_ref[0, :, sl]
        back = win // 2
        for r0 in range(0, n, rows):
            acc = pad_ref[pl.ds(POOL_HALO + r0 - back, rows), :]
            for j in range(1 - back, win - back):
                acc = acc + pad_ref[pl.ds(POOL_HALO + r0 + j, rows), :]
            t = r0 + lax.broadcasted_iota(jnp.int32, (rows, 1), 0)
            cnt = (jnp.minimum(t + (win - back), n) - jnp.maximum(t - back, 0)).astype(F32)
            m = acc / cnt - pad_ref[pl.ds(POOL_HALO + r0, rows), :]
            y = jnp.dot(m.astype(BF16), pw_ref[gi], preferred_element_type=F32)
            o_ref[0, r0:r0 + rows, sl] = (y * ps_ref[:, sl]).astype(BF16)


def _pool(p, pool_w_bf16, pool_scale):
    b, n, width = p.shape
    ng, group, _ = pool_w_bf16.shape
    kern = functools.partial(_pool_kernel, n=n, group=group)
    return pl.pallas_call(
        kern,
        out_shape=jax.ShapeDtypeStruct((b, n, width), BF16),
        grid=(b,),
        in_specs=[pl.BlockSpec((1, n, width), lambda i: (i, 0, 0)),
                  pl.BlockSpec((ng, group, group), lambda i: (0, 0, 0)),
                  pl.BlockSpec((1, width), lambda i: (0, 0))],
        out_specs=pl.BlockSpec((1, n, width), lambda i: (i, 0, 0)),
        scratch_shapes=[pltpu.VMEM((n + 2 * POOL_HALO, group), F32)],
        compiler_params=_params(1), name="pool",
    )(p, pool_w_bf16, pool_scale)


def _out0_kernel(o_ref, yp_ref, g_ref, x_ref, gate_ref, w_ref, ng_ref, sh_ref, sc_ref,
                 x1_ref, h1_ref, *, aw):
    sg = _silu(g_ref[0].astype(F32))
    a = (o_ref[0].astype(F32) * sg[:, :aw]).astype(BF16)
    c = (yp_ref[0].astype(F32) * sg[:, aw:]).astype(BF16)
    y = (jnp.dot(a, w_ref[0:aw, :], preferred_element_type=F32)
         + jnp.dot(c, w_ref[aw:, :], preferred_element_type=F32))
    x1 = x_ref[0] + gate_ref[0] * y
    x1_ref[0] = x1
    h1_ref[0] = _modulate(x1, ng_ref[...], sh_ref[0], sc_ref[0]).astype(BF16)


def _out0(o, ypool, g, x, gate, w_bf16, norm_g1, shift1, scale1):
    b, n, d = x.shape
    aw = o.shape[2]
    e = g.shape[2]
    tl = ROW_TILE
    row = lambda i, j: (i, j, 0)
    vec = lambda i, j: (i, 0, 0)
    kern = functools.partial(_out0_kernel, aw=aw)
    return pl.pallas_call(
        kern,
        out_shape=(jax.ShapeDtypeStruct((b, n, d), F32), jax.ShapeDtypeStruct((b, n, d), BF16)),
        grid=(b, n // tl),
        in_specs=[pl.BlockSpec((1, tl, aw), row), pl.BlockSpec((1, tl, e - aw), row),
                  pl.BlockSpec((1, tl, e), row), pl.BlockSpec((1, tl, d), row),
                  pl.BlockSpec((1, 1, d), vec),
                  pl.BlockSpec((e, d), lambda i, j: (0, 0)),
                  pl.BlockSpec((1, d), lambda i, j: (0, 0)),
                  pl.BlockSpec((1, 1, d), vec), pl.BlockSpec((1, 1, d), vec)],
        out_specs=(pl.BlockSpec((1, tl, d), row), pl.BlockSpec((1, tl, d), row)),
        compiler_params=_params(2), name="out0",
    )(o, ypool, g, x, gate, w_bf16, norm_g1, shift1, scale1)


def _filter_mlp_kernel(z_ref, w0, b0, f0, w1, b1, f1, w2, b2, f2, o_ref):
    def layer(h, w, b, f):
        return jnp.sin(f[...] * (jnp.dot(h, w[...], preferred_element_type=F32, precision=HIGHEST) + b[...]))
    h = layer(z_ref[...], w0, b0, f0)
    h = layer(h, w1, b1, f1)
    o_ref[...] = layer(h, w2, b2, f2)


def _filter_mlp(z, w0, b0, f0, w1, b1, f1, w2, b2, f2):
    return pl.pallas_call(
        _filter_mlp_kernel,
        out_shape=jax.ShapeDtypeStruct(z.shape, F32),
        compiler_params=pltpu.CompilerParams(vmem_limit_bytes=VMEM_LIMIT_BYTES), name="filter_mlp",
    )(z, w0, b0, f0, w1, b1, f1, w2, b2, f2)


def _aligned(start, multiple):
    return start if isinstance(start, int) else pl.multiple_of(start, multiple)


def _row_chunks(n, body, init=None):
    def step(c, carry):
        return body(pl.multiple_of(c * CONV_ROWS, CONV_ROWS), carry)
    return lax.fori_loop(0, n // CONV_ROWS, step, init)


def _dft8_half(x0, x2, x4, x6):
    s04, d04, s26, d26 = x0 + x4, x0 - x4, x2 + x6, x2 - x6
    cd, cs = SQRT_HALF * d26, SQRT_HALF * s26
    re = (s04 + s26, x0 + cd, d04, x0 - cd, s04 - s26)
    im = (None, -(x4 + cs), -d26, x4 - cs, None)
    return re, im


def _fft16_forward(x):
    er, ei = _dft8_half(x[0], x[2], x[4], x[6])
    orr, oi = _dft8_half(x[1], x[3], x[5], x[7])
    tr, ti = [None] * FFT_RES, [None] * FFT_RES
    tr[0] = er[0] + orr[0]
    tr[8] = er[0] - orr[0]
    tr[4], ti[4] = er[4], -orr[4]
    for k in (1, 2, 3):
        wr, wi = math.cos(math.pi * k / 8), -math.sin(math.pi * k / 8)
        pr = wr * orr[k] - wi * oi[k]
        pi = wr * oi[k] + wi * orr[k]
        tr[k], ti[k] = er[k] + pr, ei[k] + pi
        tr[8 - k], ti[8 - k] = er[k] - pr, pi - ei[k]
    return tr, ti


def _fft16_inverse(hr, hi):
    def quad(a0, a4, a1, a2, a3):
        (a1r, a1i), (a2r, a2i), (a3r, a3i) = a1, a2, a3
        lo, hi_ = a0 + a4, a0 - a4
        return (lo + 2.0 * (a1r + a2r + a3r),
                hi_ + 2.0 * (SQRT_HALF * ((a1r - a1i) - (a3r + a3i)) - a2i),
                lo + 2.0 * (a3i - a1i - a2r),
                hi_ + 2.0 * (SQRT_HALF * ((a3r - a3i) - (a1r + a1i)) + a2i))
    even = quad(hr[0] + hr[8], 2.0 * hr[4], *[(hr[k] + hr[8 - k], hi[k] - hi[8 - k]) for k in (1, 2, 3)])
    odd_in = []
    for k in (1, 2, 3):
        dr, di = hr[k] - hr[8 - k], hi[k] + hi[8 - k]
        wr, wi = math.cos(math.pi * k / 8), math.sin(math.pi * k / 8)
        odd_in.append((dr * wr - di * wi, dr * wi + di * wr))
    odd = quad(hr[0] - hr[8], -2.0 * hi[4], *odd_in)
    return [even[0], odd[0], even[1], odd[1], even[2], odd[2], even[3], odd[3]]


def _fft_stage1(u_ref, tw_ref, side_work=None):
    nb = FFT_BLOCK
    per_step = (nb // FFT_ROWS) // SIDE_TICKS

    def chunk(a0):
        x = [u_ref[pl.ds(_aligned(b * nb + a0, FFT_ROWS), FFT_ROWS), :] for b in range(FFT_RADIX // 2)]
        tr, ti = _fft16_forward(x)
        re_rows = pl.ds(a0, FFT_ROWS)
        im_rows = pl.ds(_aligned(nb + a0, FFT_ROWS), FFT_ROWS)
        for k in range(FFT_RES):
            tw_ref[k, re_rows, :] = tr[k].astype(BF16)
            if ti[k] is not None:
                tw_ref[k, im_rows, :] = ti[k].astype(BF16)

    if side_work is not None:
        for i in range(SIDE_TICKS):
            for j in range(per_step):
                chunk((i * per_step + j) * FFT_ROWS)
            side_work(i)
        return

    def body(i, carry):
        chunk(pl.multiple_of(i * FFT_ROWS, FFT_ROWS))
        return carry

    lax.fori_loop(0, nb // FFT_ROWS, body, None)


def _fft_forward_mxu(tw_ref, dfw_ref, k):
    if k in (0, FFT_RES - 1):
        return jnp.dot(dfw_ref[k, :, 0:FFT_BLOCK], tw_ref[k, 0:FFT_BLOCK, :], preferred_element_type=F32)
    return jnp.dot(dfw_ref[k], tw_ref[k], preferred_element_type=F32)


def _fft_stage2_inverse(h_ref, y_ref, side_work=None):
    nb = FFT_BLOCK
    n_chunks = nb // V7X_SUBLANES

    def chunk(a0):
        re_rows = pl.ds(a0, V7X_SUBLANES)
        im_rows = pl.ds(_aligned(nb + a0, V7X_SUBLANES), V7X_SUBLANES)
        hr = [h_ref[k, re_rows, :] for k in range(FFT_RES)]
        hi = [None] + [h_ref[k, im_rows, :] for k in range(1, FFT_RES - 1)] + [None]
        for b, yb in enumerate(_fft16_inverse(hr, hi)):
            y_ref[pl.ds(_aligned(b * nb + a0, V7X_SUBLANES), V7X_SUBLANES), :] = yb

    if side_work is not None:
        for i in range(SIDE_TICKS):
            for j in range(n_chunks // SIDE_TICKS):
                chunk((i * (n_chunks // SIDE_TICKS) + j) * V7X_SUBLANES)
            side_work(i)
        return

    def body(i, carry):
        chunk(pl.multiple_of(i * V7X_SUBLANES, V7X_SUBLANES))
        return carry

    lax.fori_loop(0, n_chunks, body, None)


def _filter_spec_kernel(h_ref, wf_ref, wb_ref, t_ref, dcy_ref, dfw_ref,
                        gr_ref, gi_ref, fwd_ref, bwd_ref, twf_ref, twb_ref, *, n):
    inv_n = 1.0 / (2 * n)
    row = lax.broadcasted_iota(jnp.int32, (CONV_ROWS, CH_TILE), 0)

    def taps(r0, carry):
        rows = pl.ds(r0, CONV_ROWS)
        hm = h_ref[rows, :]
        decay = jnp.exp(-t_ref[rows, :] * dcy_ref[...])
        fwd_ref[rows, :] = jnp.dot(hm, wf_ref[...], preferred_element_type=F32, precision=HIGHEST) * decay
        bwd = jnp.dot(hm, wb_ref[...], preferred_element_type=F32, precision=HIGHEST) * decay
        bwd_ref[rows, :] = jnp.where(row + r0 == 0, 0.0, bwd)
        return carry

    _row_chunks(n, taps)
    _fft_stage1(fwd_ref, twf_ref)
    _fft_stage1(bwd_ref, twb_ref)
    nb = FFT_BLOCK
    for k in range(FFT_RES):
        xf = _fft_forward_mxu(twf_ref, dfw_ref, k)
        xb = _fft_forward_mxu(twb_ref, dfw_ref, k)
        gr_ref[0, k] = (xf[0:nb] + xb[0:nb]) * inv_n
        gi_ref[0, k] = (xf[nb:2 * nb] - xb[nb:2 * nb]) * inv_n


def _filter_spec(hmlp, wout, tcol, dabs, dfw):
    n, kpad = hmlp.shape
    e = dabs.shape[1]
    ct = CH_TILE
    nct = e // ct
    nb = FFT_BLOCK
    const = lambda o, j: (0, 0)
    gshape = jax.ShapeDtypeStruct((H_ORDER, FFT_RES, nb, e), F32)
    gspec = pl.BlockSpec((1, FFT_RES, nb, ct), lambda o, j: (o, 0, 0, j))
    kern = functools.partial(_filter_spec_kernel, n=n)
    return pl.pallas_call(
        kern,
        out_shape=(gshape, gshape),
        grid=(H_ORDER, nct),
        in_specs=[pl.BlockSpec((n, kpad), const),
                  pl.BlockSpec((kpad, ct), lambda o, j: (0, (2 * o) * nct + j)),
                  pl.BlockSpec((kpad, ct), lambda o, j: (0, (2 * o + 1) * nct + j)),
                  pl.BlockSpec((n, 1), const),
                  pl.BlockSpec((1, ct), lambda o, j: (0, j)),
                  pl.BlockSpec((FFT_RES, 2 * nb, 2 * nb), lambda o, j: (0, 0, 0), pipeline_mode=pl.Buffered(1))],
        out_specs=(gspec, gspec),
        scratch_shapes=[pltpu.VMEM((n, ct), F32), pltpu.VMEM((n, ct), F32),
                        pltpu.VMEM((FFT_RES, 2 * nb, ct), BF16), pltpu.VMEM((FFT_RES, 2 * nb, ct), BF16)],
        compiler_params=_params(2), name="filter_spec",
    )(hmlp, wout, wout, tcol, dabs, dfw)


def _hyena_kernel(h_ref, w_ref, cwv_ref, cw1_ref, cw2_ref,
                  cbv_ref, cb1_ref, cb2_ref, dfw_ref, dinv_ref,
                  gr0_ref, gi0_ref, gr1_ref, gi1_ref, fb0_ref, fb1_ref,
                  o_ref, rawa_ref, rawb_ref, u_ref, tw_ref, res_ref, y_ref, *, n):
    halo = CONV_HALO
    nb = FFT_BLOCK
    ct = CH_TILE
    zero_halo = jnp.zeros((halo, 2 * ct), F32)
    for raw_ref in (rawa_ref, rawb_ref):
        raw_ref[0:halo, :] = zero_halo
        raw_ref[halo + n:2 * halo + n, :] = zero_halo

    def project_chunk(raw_ref, pair, r0):
        raw_ref[pl.ds(halo + r0, CONV_ROWS), :] = jnp.dot(
            h_ref[0, pl.ds(r0, CONV_ROWS), :], w_ref[:, 2 * pair * ct:2 * (pair + 1) * ct],
            preferred_element_type=F32)

    def short_conv(raw_ref, half, r0, cw_ref, cb_ref):
        rows = CONV_ROWS + 2 * halo
        blk = raw_ref[pl.ds(r0, rows), half * ct:(half + 1) * ct]
        prev = pltpu.roll(blk, 1, axis=0)[halo:halo + CONV_ROWS]
        cur = blk[halo:halo + CONV_ROWS]
        nxt = pltpu.roll(blk, rows - 1, axis=0)[halo:halo + CONV_ROWS]
        return cb_ref[...] + prev * cw_ref[0:1, :] + cur * cw_ref[1:2, :] + nxt * cw_ref[2:3, :]

    def long_conv(gr_ref, gi_ref, fwd_side=None, mxu_side=None, inv_side=None):
        _fft_stage1(u_ref, tw_ref, fwd_side)
        for k in range(FFT_RES):
            x = _fft_forward_mxu(tw_ref, dfw_ref, k)
            xr, xi = x[0:nb], x[nb:2 * nb]
            gr, gi = gr_ref[0, k], gi_ref[0, k]
            y = jnp.concatenate([xr * gr - xi * gi, xr * gi + xi * gr], axis=0).astype(BF16)
            if k in (0, FFT_RES - 1):
                res_ref[k, 0:nb, :] = jnp.dot(dinv_ref[k, 0:nb, :], y, preferred_element_type=F32)
            else:
                res_ref[k] = jnp.dot(dinv_ref[k], y, preferred_element_type=F32)
            if mxu_side is not None:
                mxu_side(k)
        _fft_stage2_inverse(res_ref, y_ref, inv_side)

    def stash(half, r0, val):
        rawa_ref[pl.ds(halo + r0, CONV_ROWS), half * ct:(half + 1) * ct] = val

    def conv_out_times_stashed(half, r0, fb_ref):
        rows = pl.ds(r0, CONV_ROWS)
        mult = rawa_ref[pl.ds(halo + r0, CONV_ROWS), half * ct:(half + 1) * ct]
        return mult * (y_ref[rows, :] + u_ref[rows, :] * fb_ref[0])

    def store_u(r0, u):
        u_ref[pl.ds(r0, CONV_ROWS), :] = u

    n_chunks = n // CONV_ROWS
    for c in range(n_chunks + 1):
        if c < n_chunks:
            project_chunk(rawa_ref, 0, c * CONV_ROWS)
        if c > 0:
            store_u((c - 1) * CONV_ROWS, short_conv(rawa_ref, 0, (c - 1) * CONV_ROWS, cwv_ref, cbv_ref))

    pending = [functools.partial(project_chunk, rawb_ref, 1, c * CONV_ROWS) for c in range(n_chunks)]

    def drain(ticks):
        def side(i):
            if i in ticks and pending:
                pending.pop(0)()
        return side

    def x1_conv(k):
        if k < n_chunks:
            stash(0, k * CONV_ROWS, short_conv(rawa_ref, 1, k * CONV_ROWS, cw1_ref, cb1_ref))

    def x2_conv_gate(k):
        if k < n_chunks:
            gate = rawb_ref[pl.ds(halo + k * CONV_ROWS, CONV_ROWS), ct:2 * ct]
            stash(1, k * CONV_ROWS, short_conv(rawb_ref, 0, k * CONV_ROWS, cw2_ref, cb2_ref) * _silu(gate))

    long_conv(gr0_ref, gi0_ref, fwd_side=drain((3, 7)), mxu_side=x1_conv, inv_side=drain((1, 4, 7)))
    for c in range(n_chunks):
        store_u(c * CONV_ROWS, conv_out_times_stashed(0, c * CONV_ROWS, fb0_ref))
        drain((1, 4, 7))(c)
    assert not pending
    long_conv(gr1_ref, gi1_ref, mxu_side=x2_conv_gate)

    def finish(r0, carry):
        o_ref[0, pl.ds(r0, CONV_ROWS), :] = conv_out_times_stashed(1, r0, fb1_ref).astype(BF16)
        return carry
    _row_chunks(n, finish)


def _hyena(h1, w_tiles_bf16, conv_w, conv_b, dfw, dinv, gr, gi, fbias):
    b, n, d = h1.shape
    e = gr.shape[3]
    ct = CH_TILE
    nct = e // ct
    nb = FFT_BLOCK
    cwspec = lambda k: pl.BlockSpec((SHORT_CONV, ct), lambda j, i, k=k: (0, k * nct + j))
    cbspec = lambda k: pl.BlockSpec((1, ct), lambda j, i, k=k: (0, k * nct + j))
    dspec = pl.BlockSpec((FFT_RES, 2 * nb, 2 * nb), lambda j, i: (0, 0, 0), pipeline_mode=pl.Buffered(1))
    gspec = lambda o: pl.BlockSpec((1, FFT_RES, nb, ct), lambda j, i, o=o: (o, 0, 0, j),
                                   pipeline_mode=pl.Buffered(1))
    vspec = lambda o: pl.BlockSpec((1, 1, ct), lambda j, i, o=o: (o, 0, j))
    kern = functools.partial(_hyena_kernel, n=n)
    return pl.pallas_call(
        kern,
        out_shape=jax.ShapeDtypeStruct((b, n, e), BF16),
        grid=(nct, b),
        in_specs=[pl.BlockSpec((1, n, d), lambda j, i: (i, 0, 0)),
                  pl.BlockSpec((d, 4 * ct), lambda j, i: (0, j)),
                  cwspec(0), cwspec(1), cwspec(2), cbspec(0), cbspec(1), cbspec(2),
                  dspec, dspec,
                  gspec(0), gspec(0), gspec(1), gspec(1), vspec(0), vspec(1)],
        out_specs=pl.BlockSpec((1, n, ct), lambda j, i: (i, 0, j)),
        scratch_shapes=[pltpu.VMEM((n + 2 * CONV_HALO, 2 * ct), F32), pltpu.VMEM((n + 2 * CONV_HALO, 2 * ct), F32),
                        pltpu.VMEM((n, ct), F32),
                        pltpu.VMEM((FFT_RES, 2 * nb, ct), BF16), pltpu.VMEM((FFT_RES, 2 * nb, ct), F32),
                        pltpu.VMEM((n, ct), F32)],
        compiler_params=_params(2), name="hyena_mix",
    )(h1, w_tiles_bf16, conv_w, conv_w, conv_w, conv_b, conv_b, conv_b,
      dfw, dinv, gr, gi, gr, gi, fbias, fbias)


def _out1_kernel(z_ref, x_ref, gate_ref, w_ref, fg_ref, o_ref):
    y = jnp.dot(z_ref[0], w_ref[...], preferred_element_type=F32)
    x2 = x_ref[0] + gate_ref[0] * y
    o_ref[0] = (x2 * lax.rsqrt(jnp.mean(x2 * x2, axis=-1, keepdims=True) + RMS_EPS)) * fg_ref[...]


def _out1(zg, x1, gate, w_bf16, final_g):
    b, n, d = x1.shape
    e = zg.shape[2]
    tl = ROW_TILE
    row = lambda i, j: (i, j, 0)
    return pl.pallas_call(
        _out1_kernel,
        out_shape=jax.ShapeDtypeStruct((b, n, d), F32),
        grid=(b, n // tl),
        in_specs=[pl.BlockSpec((1, tl, e), row), pl.BlockSpec((1, tl, d), row),
                  pl.BlockSpec((1, 1, d), lambda i, j: (i, 0, 0)),
                  pl.BlockSpec((e, d), lambda i, j: (0, 0)),
                  pl.BlockSpec((1, d), lambda i, j: (0, 0))],
        out_specs=pl.BlockSpec((1, tl, d), row),
        compiler_params=_params(2), name="out1",
    )(zg, x1, gate, w_bf16, final_g)


def _rope_tables(n):
    rows = n // GRID_W
    row = jnp.repeat(jnp.arange(rows), GRID_W).astype(F32)
    col = jnp.tile(jnp.arange(GRID_W), rows).astype(F32)
    half = A_HEAD_DIM // 2
    inv = ROPE_THETA ** (-jnp.arange(0, half, 2, dtype=F32) / half)
    ar, ac = row[:, None] * inv, col[:, None] * inv
    ang = jnp.concatenate([ar, ar, ac, ac], axis=-1)
    cos, sin = jnp.cos(ang), jnp.sin(ang)
    first = (jnp.arange(A_HEAD_DIM) % (half)) < (half // 2)
    sin_lo = jnp.where(first, -sin, 0.0)
    sin_hi = jnp.where(first, 0.0, sin)
    rep = V7X_LANES // A_HEAD_DIM
    return jnp.tile(cos, (1, rep)), jnp.tile(sin_lo, (1, rep)), jnp.tile(sin_hi, (1, rep))


def _fft_tables(n):
    nb = FFT_BLOCK
    idx = jnp.arange(nb, dtype=jnp.int32)
    k2 = jnp.arange(FFT_RES, dtype=jnp.int32)
    freq = FFT_RADIX * idx[None, :, None] + k2[:, None, None]
    ang = ((freq * idx[None, None, :]) % (2 * n)).astype(F32) * (math.pi / n)
    c, s = jnp.cos(ang), jnp.sin(ang)
    dfw = jnp.concatenate([jnp.concatenate([c, s], axis=2), jnp.concatenate([-s, c], axis=2)], axis=1)
    return dfw.astype(BF16), jnp.swapaxes(dfw, 1, 2).astype(BF16)


def _filter_features(n, kpad):
    t = jnp.linspace(0.0, 1.0, n, dtype=F32)[:, None]
    w = 2.0 * math.pi * jnp.arange(n, dtype=F32)[:, None] / n
    bands = jnp.linspace(1e-4, FILTER_BANDS - 1, FILTER_BANDS, dtype=F32)[None, :]
    z = jnp.concatenate([t, jnp.cos(bands * w), -jnp.sin(bands * w)], axis=-1)
    return jnp.pad(z, ((0, 0), (0, kpad - z.shape[1]))), t


def _pad2(a, rows, cols):
    return jnp.pad(a, ((0, rows - a.shape[0]), (0, cols - a.shape[1])))


def kernel(x, c, ctx, c_ctx, norm_g, ada_w, ada_b, final_g, a_w_in, a_lam_q1, a_lam_k1, a_lam_q2, a_lam_k2, a_subln_g, a_pool_w, a_pool_scale, a_w_out, h_w_in, h_conv_w, h_conv_b, h_filt_w0, h_filt_b0, h_filt_f0, h_filt_w1, h_filt_b1, h_filt_f1, h_filt_w2, h_filt_b2, h_filt_f2, h_filt_wout, h_filt_bias, h_w_out):
    b, n, d = x.shape
    nc = ctx.shape[1]
    e = h_w_out.shape[1]
    assert norm_g.shape[0] == 2 and a_w_in.shape[0] == 1 and h_w_in.shape[0] == 1
    assert b + 1 <= COND_ROWS and n % ROW_TILE == 0 and n % Q_TILE == 0 and (b * nc) % ROW_TILE == 0
    assert d == A_HEADS * A_V_DIM and e % CH_TILE == 0 and 2 * n == FFT_BLOCK * FFT_RADIX

    cond = jnp.concatenate([c, c_ctx[None, :], jnp.zeros((COND_ROWS - b - 1, d), F32)], axis=0)
    mods = _ada(cond, ada_w, ada_b)
    shift = mods[:, :, 0:d]
    scale = mods[:, :, d:2 * d]
    gate = mods[:, :, 2 * d:3 * d]
    per_batch = lambda m, i: m[i, :b].reshape(b, 1, d)

    lam_init = 0.8 - 0.6 * math.exp(-0.3 * 0)
    w0 = a_w_in[0].astype(BF16)
    cos, sin_lo, sin_hi = _rope_tables(n)
    wvt = a_w_in[0, :, 2 * d:3 * d].T.astype(BF16)
    q, k, vt, p, g = _proj0(x, norm_g[0:1], per_batch(shift, 0), per_batch(scale, 0), cos, sin_lo, sin_hi,
                            w0, wvt)
    kc, vct = _ctx_kv(ctx.reshape(b * nc, d), norm_g[0:1], shift[0, b:b + 1], scale[0, b:b + 1], w0, wvt)
    o = _attention(q, kc.reshape(b, nc, d), k, vct, vt,
                   a_lam_q1, a_lam_k1, a_lam_q2, a_lam_k2, a_subln_g, lam_init)
    ypool = _pool(p, a_pool_w[0].astype(BF16), a_pool_scale)
    x1, h1 = _out0(o, ypool, g, x, per_batch(gate, 0), a_w_out[0].astype(BF16),
                   norm_g[1:2], per_batch(shift, 1), per_batch(scale, 1))

    kpad = V7X_LANES
    z, tcol = _filter_features(n, kpad)
    row1 = lambda a: _pad2(a, 1, kpad)
    hmlp = _filter_mlp(z, _pad2(h_filt_w0[0], kpad, kpad), row1(h_filt_b0), row1(h_filt_f0),
                       _pad2(h_filt_w1[0], kpad, kpad), row1(h_filt_b1), row1(h_filt_f1),
                       _pad2(h_filt_w2[0], kpad, kpad), row1(h_filt_b2), row1(h_filt_f2))
    max_decay = math.log(DECAY_TARGET) / FAST_DECAY_PCT
    min_decay = math.log(DECAY_TARGET) / SLOW_DECAY_PCT
    dabs = jnp.abs(jnp.linspace(min_decay, max_decay, e, dtype=F32))[None, :]
    dfw, dinv = _fft_tables(n)
    wout = _pad2(h_filt_wout[0], kpad, h_filt_wout.shape[2])
    gr, gi = _filter_spec(hmlp, wout, tcol, dabs, dfw)
    nct = e // CH_TILE
    w_tiles = (h_w_in[0].astype(BF16).reshape(d, H_ORDER + 2, nct, CH_TILE)
               .transpose(0, 2, 1, 3).reshape(d, (H_ORDER + 2) * e))
    zg = _hyena(h1, w_tiles, h_conv_w[0], h_conv_b, dfw, dinv, gr, gi,
                h_filt_bias[0].reshape(H_ORDER, 1, e))
    return _out1(zg, x1, per_batch(gate, 1), h_w_out[0].astype(BF16), final_g[None, :])
```

```python
import functools
import math

import jax
import jax.numpy as jnp
from jax import lax
from jax.experimental import pallas as pl
from jax.experimental.pallas import tpu as pltpu

F32 = jnp.float32
BF16 = jnp.bfloat16
HIGHEST = lax.Precision.HIGHEST

GRID_W = 64
A_HEADS = 8
A_HEAD_DIM = 64
A_V_DIM = 2 * A_HEAD_DIM
POOL_WINDOWS = (2, 4, 8, 16)
ROPE_THETA = 10000.0
H_ORDER = 2
SHORT_CONV = 3
FILTER_EMB = 33
FILTER_BANDS = (FILTER_EMB - 1) // 2
DECAY_TARGET = 1e-2
FAST_DECAY_PCT = 0.3
SLOW_DECAY_PCT = 1.5
RMS_EPS = 1e-6
SUBLN_EPS = 1e-5
LOG2E = 1.4426950408889634

V7X_LANES = 128
V7X_SUBLANES = 8
V7X_MXU_DIM = 256
V7X_VMEM_BYTES = 64 * 1024 * 1024
VMEM_LIMIT_BYTES = V7X_VMEM_BYTES - 6 * 1024 * 1024

ROW_TILE = 512
Q_TILE = 2048
Q_SUB = V7X_MXU_DIM
KEY_CHUNK = V7X_MXU_DIM
CH_TILE = V7X_MXU_DIM
POOL_HALO = V7X_SUBLANES
CONV_ROWS = 256
CONV_HALO = V7X_SUBLANES
FFT_BLOCK = V7X_MXU_DIM
FFT_RADIX = 16
FFT_RES = FFT_RADIX // 2 + 1
FFT_ROWS = 16
SIDE_TICKS = 8
SQRT_HALF = math.sqrt(0.5)
COND_ROWS = 24


def _params(n_axes):
    return pltpu.CompilerParams(dimension_semantics=("arbitrary",) * n_axes,
                                vmem_limit_bytes=VMEM_LIMIT_BYTES)


def _silu(v):
    half = 0.5 * v
    return half + half * jnp.tanh(half)


def _modulate(x, g, shift, scale):
    y = x * lax.rsqrt(jnp.mean(x * x, axis=-1, keepdims=True) + RMS_EPS)
    return (y * g) * (1.0 + scale) + shift


def _nt_dot(a, b):
    return lax.dot_general(a, b, (((1,), (1,)), ((), ())), preferred_element_type=F32)


def _ada_kernel(cond_ref, w_ref, b_ref, o_ref):
    s = _silu(cond_ref[...])
    o_ref[0] = jnp.dot(s, w_ref[0], preferred_element_type=F32, precision=HIGHEST) + b_ref[0]


def _ada(cond, ada_w, ada_b):
    depth, d, d3 = ada_w.shape
    nt = d3 // d
    return pl.pallas_call(
        _ada_kernel,
        out_shape=jax.ShapeDtypeStruct((depth, COND_ROWS, d3), F32),
        grid=(depth, nt),
        in_specs=[pl.BlockSpec((COND_ROWS, d), lambda i, j: (0, 0)),
                  pl.BlockSpec((1, d, d), lambda i, j: (i, 0, j)),
                  pl.BlockSpec((1, 1, d), lambda i, j: (i, 0, j))],
        out_specs=pl.BlockSpec((1, COND_ROWS, d), lambda i, j: (i, 0, j)),
        compiler_params=_params(2), name="ada",
    )(cond, ada_w, ada_b.reshape(depth, 1, d3))


def _rope_slab(t, cos, sin_lo, sin_hi):
    return (t * cos + pltpu.roll(t, V7X_LANES - 16, axis=1) * sin_lo
            + pltpu.roll(t, 16, axis=1) * sin_hi)


def _proj0_kernel(x_ref, ng_ref, sh_ref, sc_ref, cos_ref, slo_ref, shi_ref, w_ref, wvt_ref,
                  q_ref, k_ref, vt_ref, p_ref, g_ref, *, d, q_scale):
    hb = _modulate(x_ref[0], ng_ref[...], sh_ref[0], sc_ref[0]).astype(BF16)
    cos, slo, shi = cos_ref[...], slo_ref[...], shi_ref[...]

    def proj(c0, width):
        return jnp.dot(hb, w_ref[:, c0:c0 + width], preferred_element_type=F32)

    def rope(t, scale):
        for s in range(d // V7X_LANES):
            sl = slice(s * V7X_LANES, (s + 1) * V7X_LANES)
            yield sl, _rope_slab(t[:, sl], cos, slo, shi) * scale

    for sl, r in rope(proj(0, d), q_scale):
        q_ref[0, :, sl] = r.astype(BF16)
    for sl, r in rope(proj(d, d), 1.0):
        k_ref[0, :, sl] = r.astype(BF16)
    vt_ref[0] = _nt_dot(wvt_ref[...], hb).astype(BF16)
    p_ref[0] = proj(3 * d, d)
    g_ref[0] = proj(4 * d, 2 * d).astype(BF16)


def _proj0(x, norm_g, shift, scale, cos, sin_lo, sin_hi, w_bf16, wvt_bf16):
    b, n, d = x.shape
    ncols = w_bf16.shape[1]
    tl = ROW_TILE
    row = lambda j, i: (i, j, 0)
    vec = lambda j, i: (i, 0, 0)
    tab = lambda j, i: (j, 0)
    kern = functools.partial(_proj0_kernel, d=d, q_scale=A_HEAD_DIM ** -0.5 * LOG2E)
    return pl.pallas_call(
        kern,
        out_shape=(jax.ShapeDtypeStruct((b, n, d), BF16), jax.ShapeDtypeStruct((b, n, d), BF16),
                   jax.ShapeDtypeStruct((b, d, n), BF16), jax.ShapeDtypeStruct((b, n, d), F32),
                   jax.ShapeDtypeStruct((b, n, 2 * d), BF16)),
        grid=(n // tl, b),
        in_specs=[pl.BlockSpec((1, tl, d), row),
                  pl.BlockSpec((1, d), lambda j, i: (0, 0)),
                  pl.BlockSpec((1, 1, d), vec), pl.BlockSpec((1, 1, d), vec),
                  pl.BlockSpec((tl, V7X_LANES), tab), pl.BlockSpec((tl, V7X_LANES), tab),
                  pl.BlockSpec((tl, V7X_LANES), tab),
                  pl.BlockSpec((d, ncols), lambda j, i: (0, 0), pipeline_mode=pl.Buffered(1)),
                  pl.BlockSpec((d, d), lambda j, i: (0, 0), pipeline_mode=pl.Buffered(1))],
        out_specs=(pl.BlockSpec((1, tl, d), row), pl.BlockSpec((1, tl, d), row),
                   pl.BlockSpec((1, d, tl), lambda j, i: (i, 0, j)), pl.BlockSpec((1, tl, d), row),
                   pl.BlockSpec((1, tl, 2 * d), row)),
        compiler_params=_params(2), name="proj0",
    )(x, norm_g, shift, scale, cos, sin_lo, sin_hi, w_bf16, wvt_bf16)


def _ctx_kv_kernel(c_ref, ng_ref, sh_ref, sc_ref, wk_ref, wvt_ref, k_ref, vt_ref):
    hb = _modulate(c_ref[...], ng_ref[...], sh_ref[...], sc_ref[...]).astype(BF16)
    k_ref[...] = jnp.dot(hb, wk_ref[...], preferred_element_type=F32).astype(BF16)
    vt_ref[...] = _nt_dot(wvt_ref[...], hb).astype(BF16)


def _ctx_kv(ctx_rows, norm_g, shift_c, scale_c, w_bf16, wvt_bf16):
    rows, d = ctx_rows.shape
    tr = ROW_TILE
    one = lambda i: (0, 0)
    return pl.pallas_call(
        _ctx_kv_kernel,
        out_shape=(jax.ShapeDtypeStruct((rows, d), BF16), jax.ShapeDtypeStruct((d, rows), BF16)),
        grid=(rows // tr,),
        in_specs=[pl.BlockSpec((tr, d), lambda i: (i, 0)),
                  pl.BlockSpec((1, d), one), pl.BlockSpec((1, d), one), pl.BlockSpec((1, d), one),
                  pl.BlockSpec((d, d), lambda i: (0, 1)), pl.BlockSpec((d, d), one)],
        out_specs=(pl.BlockSpec((tr, d), lambda i: (i, 0)), pl.BlockSpec((d, tr), lambda i: (0, i))),
        compiler_params=_params(1), name="ctx_kv",
    )(ctx_rows, norm_g, shift_c, scale_c, w_bf16, wvt_bf16)


def _attn_kernel(q_ref, kc_ref, k_ref, vct_ref, vt_ref, lq1_ref, lk1_ref, lq2_ref, lk2_ref, sg_ref,
                 o_ref, s_ref, *, lam_init):
    n_sub = q_ref.shape[1] // Q_SUB
    n_chunk = (kc_ref.shape[1] + k_ref.shape[1]) // KEY_CHUNK
    n_ctx = kc_ref.shape[1] // KEY_CHUNK
    lam = (jnp.exp(jnp.sum(lq1_ref[...] * lk1_ref[...], axis=-1, keepdims=True))
           - jnp.exp(jnp.sum(lq2_ref[...] * lk2_ref[...], axis=-1, keepdims=True)) + lam_init)
    lane = lax.broadcasted_iota(jnp.int32, (Q_SUB, A_V_DIM), 1)
    masks = (lane < A_HEAD_DIM, lane >= A_HEAD_DIM)

    def keys(j):
        if j < n_ctx:
            return kc_ref[0, j * KEY_CHUNK:(j + 1) * KEY_CHUNK, :]
        return k_ref[0, (j - n_ctx) * KEY_CHUNK:(j - n_ctx + 1) * KEY_CHUNK, :]

    def values_t(j):
        if j < n_ctx:
            return vct_ref[:, j * KEY_CHUNK:(j + 1) * KEY_CHUNK]
        return vt_ref[0, :, (j - n_ctx) * KEY_CHUNK:(j - n_ctx + 1) * KEY_CHUNK]

    def fold(x, op):
        return op(x.reshape(x.shape[0] // V7X_SUBLANES, V7X_SUBLANES, Q_SUB), axis=0)

    state = {}

    def score_task(t, c, j):
        if (t, c, "q") not in state:
            q = q_ref[0, t * Q_SUB:(t + 1) * Q_SUB, :]
            state[t, c, "q"] = jnp.where(masks[c], q, jnp.zeros_like(q))
        s = _nt_dot(keys(j), state[t, c, "q"])
        s_ref[t % 2, c, j * KEY_CHUNK:(j + 1) * KEY_CHUNK, :] = s
        m8 = fold(s, jnp.max)
        state[t, c, "m8"] = m8 if j == 0 else jnp.maximum(state[t, c, "m8"], m8)

    def exp_task(t, c, j):
        if j == 0:
            state[t, c, "m"] = jnp.max(state[t, c, "m8"], axis=0, keepdims=True)
        p = jnp.exp2(s_ref[t % 2, c, j * KEY_CHUNK:(j + 1) * KEY_CHUNK, :] - state[t, c, "m"])
        l8 = fold(p, jnp.sum)
        pv = jnp.dot(values_t(j), p.astype(BF16), preferred_element_type=F32)
        state[t, c, "l8"] = l8 if j == 0 else state[t, c, "l8"] + l8
        state[t, c, "acc"] = pv if j == 0 else state[t, c, "acc"] + pv

    def finish(t):
        l1 = jnp.sum(state[t, 0, "l8"], axis=0, keepdims=True)
        l2 = jnp.sum(state[t, 1, "l8"], axis=0, keepdims=True)
        ot = state[t, 0, "acc"] * (1.0 / l1) - state[t, 1, "acc"] * (lam / l2)
        on = ot * lax.rsqrt(jnp.mean(ot * ot, axis=0, keepdims=True) + SUBLN_EPS)
        o_ref[0, t * Q_SUB:(t + 1) * Q_SUB, :] = ((on.T * sg_ref[...]) * (1.0 - lam_init)).astype(BF16)

    order = [(c, j) for c in range(2) for j in range(n_chunk)]
    for t in range(n_sub + 1):
        for c, j in order:
            if t < n_sub:
                score_task(t, c, j)
            if t > 0:
                exp_task(t - 1, c, j)
        if t > 0:
            finish(t - 1)


def _attention(q, kc, k, vct, vt, lq1, lk1, lq2, lk2, subln_g, lam_init):
    b, n, d = q.shape
    nc = kc.shape[1]
    hd = A_V_DIM
    tq = Q_TILE
    kv = lambda bi, h, i: (bi, 0, h)
    one = lambda bi, h, i: (0, 0)
    kern = functools.partial(_attn_kernel, lam_init=lam_init)
    return pl.pallas_call(
        kern,
        out_shape=jax.ShapeDtypeStruct((b, n, d), BF16),
        grid=(b, A_HEADS, n // tq),
        in_specs=[pl.BlockSpec((1, tq, hd), lambda bi, h, i: (bi, i, h)),
                  pl.BlockSpec((1, nc, hd), kv), pl.BlockSpec((1, n, hd), kv),
                  pl.BlockSpec((hd, nc), lambda bi, h, i: (h, bi)),
                  pl.BlockSpec((1, hd, n), lambda bi, h, i: (bi, h, 0)),
                  pl.BlockSpec((1, A_HEAD_DIM), one), pl.BlockSpec((1, A_HEAD_DIM), one),
                  pl.BlockSpec((1, A_HEAD_DIM), one), pl.BlockSpec((1, A_HEAD_DIM), one),
                  pl.BlockSpec((1, hd), one)],
        out_specs=pl.BlockSpec((1, tq, hd), lambda bi, h, i: (bi, i, h)),
        scratch_shapes=[pltpu.VMEM((2, 2, nc + n, Q_SUB), F32)],
        compiler_params=_params(3), name="diff_attn",
    )(q, kc, k, vct, vt, lq1, lk1, lq2, lk2, subln_g)


def _pool_kernel(p_ref, pw_ref, ps_ref, o_ref, pad_ref, *, n, group):
    rows = ROW_TILE
    zeros = jnp.zeros((POOL_HALO, group), F32)
    for gi, win in enumerate(POOL_WINDOWS):
        sl = slice(gi * group, (gi + 1) * group)
        pad_ref[0:POOL_HALO, :] = zeros
        pad_ref[POOL_HALO + n:2 * POOL_HALO + n, :] = zeros
        pad_ref[POOL_HALO:POOL_HALO + n, :] = p_ref[0, :, sl]
        back = win // 2
        for r0 in range(0, n, rows):
            acc = pad_ref[pl.ds(POOL_HALO + r0 - back, rows), :]
            for j in range(1 - back, win - back):
                acc = acc + pad_ref[pl.ds(POOL_HALO + r0 + j, rows), :]
            t = r0 + lax.broadcasted_iota(jnp.int32, (rows, 1), 0)
            cnt = (jnp.minimum(t + (win - back), n) - jnp.maximum(t - back, 0)).astype(F32)
            m = acc / cnt - pad_ref[pl.ds(POOL_HALO + r0, rows), :]
            y = jnp.dot(m.astype(BF16), pw_ref[gi], preferred_element_type=F32)
            o_ref[0, r0:r0 + rows, sl] = (y * ps_ref[:, sl]).astype(BF16)


def _pool(p, pool_w_bf16, pool_scale):
    b, n, width = p.shape
    ng, group, _ = pool_w_bf16.shape
    kern = functools.partial(_pool_kernel, n=n, group=group)
    return pl.pallas_call(
        kern,
        out_shape=jax.ShapeDtypeStruct((b, n, width), BF16),
        grid=(b,),
        in_specs=[pl.BlockSpec((1, n, width), lambda i: (i, 0, 0)),
                  pl.BlockSpec((ng, group, group), lambda i: (0, 0, 0)),
                  pl.BlockSpec((1, width), lambda i: (0, 0))],
        out_specs=pl.BlockSpec((1, n, width), lambda i: (i, 0, 0)),
        scratch_shapes=[pltpu.VMEM((n + 2 * POOL_HALO, group), F32)],
        compiler_params=_params(1), name="pool",
    )(p, pool_w_bf16, pool_scale)


def _out0_kernel(o_ref, yp_ref, g_ref, x_ref, gate_ref, w_ref, ng_ref, sh_ref, sc_ref,
                 x1_ref, h1_ref, *, aw):
    sg = _silu(g_ref[0].astype(F32))
    a = (o_ref[0].astype(F32) * sg[:, :aw]).astype(BF16)
    c = (yp_ref[0].astype(F32) * sg[:, aw:]).astype(BF16)
    y = (jnp.dot(a, w_ref[0:aw, :], preferred_element_type=F32)
         + jnp.dot(c, w_ref[aw:, :], preferred_element_type=F32))
    x1 = x_ref[0] + gate_ref[0] * y
    x1_ref[0] = x1
    h1_ref[0] = _modulate(x1, ng_ref[...], sh_ref[0], sc_ref[0]).astype(BF16)


def _out0(o, ypool, g, x, gate, w_bf16, norm_g1, shift1, scale1):
    b, n, d = x.shape
    aw = o.shape[2]
    e = g.shape[2]
    tl = ROW_TILE
    row = lambda i, j: (i, j, 0)
    vec = lambda i, j: (i, 0, 0)
    kern = functools.partial(_out0_kernel, aw=aw)
    return pl.pallas_call(
        kern,
        out_shape=(jax.ShapeDtypeStruct((b, n, d), F32), jax.ShapeDtypeStruct((b, n, d), BF16)),
        grid=(b, n // tl),
        in_specs=[pl.BlockSpec((1, tl, aw), row), pl.BlockSpec((1, tl, e - aw), row),
                  pl.BlockSpec((1, tl, e), row), pl.BlockSpec((1, tl, d), row),
                  pl.BlockSpec((1, 1, d), vec),
                  pl.BlockSpec((e, d), lambda i, j: (0, 0)),
                  pl.BlockSpec((1, d), lambda i, j: (0, 0)),
                  pl.BlockSpec((1, 1, d), vec), pl.BlockSpec((1, 1, d), vec)],
        out_specs=(pl.BlockSpec((1, tl, d), row), pl.BlockSpec((1, tl, d), row)),
        compiler_params=_params(2), name="out0",
    )(o, ypool, g, x, gate, w_bf16, norm_g1, shift1, scale1)


def _filter_mlp_kernel(z_ref, w0, b0, f0, w1, b1, f1, w2, b2, f2, o_ref):
    def layer(h, w, b, f):
        return jnp.sin(f[...] * (jnp.dot(h, w[...], preferred_element_type=F32, precision=HIGHEST) + b[...]))
    h = layer(z_ref[...], w0, b0, f0)
    h = layer(h, w1, b1, f1)
    o_ref[...] = layer(h, w2, b2, f2)


def _filter_mlp(z, w0, b0, f0, w1, b1, f1, w2, b2, f2):
    return pl.pallas_call(
        _filter_mlp_kernel,
        out_shape=jax.ShapeDtypeStruct(z.shape, F32),
        compiler_params=pltpu.CompilerParams(vmem_limit_bytes=VMEM_LIMIT_BYTES), name="filter_mlp",
    )(z, w0, b0, f0, w1, b1, f1, w2, b2, f2)


def _aligned(start, multiple):
    return start if isinstance(start, int) else pl.multiple_of(start, multiple)


def _row_chunks(n, body, init=None):
    def step(c, carry):
        return body(pl.multiple_of(c * CONV_ROWS, CONV_ROWS), carry)
    return lax.fori_loop(0, n // CONV_ROWS, step, init)


def _dft8_half(x0, x2, x4, x6):
    s04, d04, s26, d26 = x0 + x4, x0 - x4, x2 + x6, x2 - x6
    cd, cs = SQRT_HALF * d26, SQRT_HALF * s26
    re = (s04 + s26, x0 + cd, d04, x0 - cd, s04 - s26)
    im = (None, -(x4 + cs), -d26, x4 - cs, None)
    return re, im


def _fft16_forward(x):
    er, ei = _dft8_half(x[0], x[2], x[4], x[6])
    orr, oi = _dft8_half(x[1], x[3], x[5], x[7])
    tr, ti = [None] * FFT_RES, [None] * FFT_RES
    tr[0] = er[0] + orr[0]
    tr[8] = er[0] - orr[0]
    tr[4], ti[4] = er[4], -orr[4]
    for k in (1, 2, 3):
        wr, wi = math.cos(math.pi * k / 8), -math.sin(math.pi * k / 8)
        pr = wr * orr[k] - wi * oi[k]
        pi = wr * oi[k] + wi * orr[k]
        tr[k], ti[k] = er[k] + pr, ei[k] + pi
        tr[8 - k], ti[8 - k] = er[k] - pr, pi - ei[k]
    return tr, ti


def _fft16_inverse(hr, hi):
    def quad(a0, a4, a1, a2, a3):
        (a1r, a1i), (a2r, a2i), (a3r, a3i) = a1, a2, a3
        lo, hi_ = a0 + a4, a0 - a4
        return (lo + 2.0 * (a1r + a2r + a3r),
                hi_ + 2.0 * (SQRT_HALF * ((a1r - a1i) - (a3r + a3i)) - a2i),
                lo + 2.0 * (a3i - a1i - a2r),
                hi_ + 2.0 * (SQRT_HALF * ((a3r - a3i) - (a1r + a1i)) + a2i))
    even = quad(hr[0] + hr[8], 2.0 * hr[4], *[(hr[k] + hr[8 - k], hi[k] - hi[8 - k]) for k in (1, 2, 3)])
    odd_in = []
    for k in (1, 2, 3):
        dr, di = hr[k] - hr[8 - k], hi[k] + hi[8 - k]
        wr, wi = math.cos(math.pi * k / 8), math.sin(math.pi * k / 8)
        odd_in.append((dr * wr - di * wi, dr * wi + di * wr))
    odd = quad(hr[0] - hr[8], -2.0 * hi[4], *odd_in)
    return [even[0], odd[0], even[1], odd[1], even[2], odd[2], even[3], odd[3]]


def _fft_stage1(u_ref, tw_ref, side_work=None):
    nb = FFT_BLOCK
    per_step = (nb // FFT_ROWS) // SIDE_TICKS

    def chunk(a0):
        x = [u_ref[pl.ds(_aligned(b * nb + a0, FFT_ROWS), FFT_ROWS), :] for b in range(FFT_RADIX // 2)]
        tr, ti = _fft16_forward(x)
        re_rows = pl.ds(a0, FFT_ROWS)
        im_rows = pl.ds(_aligned(nb + a0, FFT_ROWS), FFT_ROWS)
        for k in range(FFT_RES):
            tw_ref[k, re_rows, :] = tr[k].astype(BF16)
            if ti[k] is not None:
                tw_ref[k, im_rows, :] = ti[k].astype(BF16)

    if side_work is not None:
        for i in range(SIDE_TICKS):
            for j in range(per_step):
                chunk((i * per_step + j) * FFT_ROWS)
            side_work(i)
        return

    def body(i, carry):
        chunk(pl.multiple_of(i * FFT_ROWS, FFT_ROWS))
        return carry

    lax.fori_loop(0, nb // FFT_ROWS, body, None)


def _fft_forward_mxu(tw_ref, dfw_ref, k):
    if k in (0, FFT_RES - 1):
        return jnp.dot(dfw_ref[k, :, 0:FFT_BLOCK], tw_ref[k, 0:FFT_BLOCK, :], preferred_element_type=F32)
    return jnp.dot(dfw_ref[k], tw_ref[k], preferred_element_type=F32)


def _fft_stage2_inverse(h_ref, y_ref, side_work=None):
    nb = FFT_BLOCK
    n_chunks = nb // V7X_SUBLANES

    def chunk(a0):
        re_rows = pl.ds(a0, V7X_SUBLANES)
        im_rows = pl.ds(_aligned(nb + a0, V7X_SUBLANES), V7X_SUBLANES)
        hr = [h_ref[k, re_rows, :] for k in range(FFT_RES)]
        hi = [None] + [h_ref[k, im_rows, :] for k in range(1, FFT_RES - 1)] + [None]
        for b, yb in enumerate(_fft16_inverse(hr, hi)):
            y_ref[pl.ds(_aligned(b * nb + a0, V7X_SUBLANES), V7X_SUBLANES), :] = yb

    if side_work is not None:
        for i in range(SIDE_TICKS):
            for j in range(n_chunks // SIDE_TICKS):
                chunk((i * (n_chunks // SIDE_TICKS) + j) * V7X_SUBLANES)
            side_work(i)
        return

    def body(i, carry):
        chunk(pl.multiple_of(i * V7X_SUBLANES, V7X_SUBLANES))
        return carry

    lax.fori_loop(0, n_chunks, body, None)


def _filter_spec_kernel(h_ref, wf_ref, wb_ref, t_ref, dcy_ref, dfw_ref,
                        gr_ref, gi_ref, fwd_ref, bwd_ref, twf_ref, twb_ref, *, n):
    inv_n = 1.0 / (2 * n)
    row = lax.broadcasted_iota(jnp.int32, (CONV_ROWS, CH_TILE), 0)

    def taps(r0, carry):
        rows = pl.ds(r0, CONV_ROWS)
        hm = h_ref[rows, :]
        decay = jnp.exp(-t_ref[rows, :] * dcy_ref[...])
        fwd_ref[rows, :] = jnp.dot(hm, wf_ref[...], preferred_element_type=F32, precision=HIGHEST) * decay
        bwd = jnp.dot(hm, wb_ref[...], preferred_element_type=F32, precision=HIGHEST) * decay
        bwd_ref[rows, :] = jnp.where(row + r0 == 0, 0.0, bwd)
        return carry

    _row_chunks(n, taps)
    _fft_stage1(fwd_ref, twf_ref)
    _fft_stage1(bwd_ref, twb_ref)
    nb = FFT_BLOCK
    for k in range(FFT_RES):
        xf = _fft_forward_mxu(twf_ref, dfw_ref, k)
        xb = _fft_forward_mxu(twb_ref, dfw_ref, k)
        gr_ref[0, k] = (xf[0:nb] + xb[0:nb]) * inv_n
        gi_ref[0, k] = (xf[nb:2 * nb] - xb[nb:2 * nb]) * inv_n


def _filter_spec(hmlp, wout, tcol, dabs, dfw):
    n, kpad = hmlp.shape
    e = dabs.shape[1]
    ct = CH_TILE
    nct = e // ct
    nb = FFT_BLOCK
    const = lambda o, j: (0, 0)
    gshape = jax.ShapeDtypeStruct((H_ORDER, FFT_RES, nb, e), F32)
    gspec = pl.BlockSpec((1, FFT_RES, nb, ct), lambda o, j: (o, 0, 0, j))
    kern = functools.partial(_filter_spec_kernel, n=n)
    return pl.pallas_call(
        kern,
        out_shape=(gshape, gshape),
        grid=(H_ORDER, nct),
        in_specs=[pl.BlockSpec((n, kpad), const),
                  pl.BlockSpec((kpad, ct), lambda o, j: (0, (2 * o) * nct + j)),
                  pl.BlockSpec((kpad, ct), lambda o, j: (0, (2 * o + 1) * nct + j)),
                  pl.BlockSpec((n, 1), const),
                  pl.BlockSpec((1, ct), lambda o, j: (0, j)),
                  pl.BlockSpec((FFT_RES, 2 * nb, 2 * nb), lambda o, j: (0, 0, 0), pipeline_mode=pl.Buffered(1))],
        out_specs=(gspec, gspec),
        scratch_shapes=[pltpu.VMEM((n, ct), F32), pltpu.VMEM((n, ct), F32),
                        pltpu.VMEM((FFT_RES, 2 * nb, ct), BF16), pltpu.VMEM((FFT_RES, 2 * nb, ct), BF16)],
        compiler_params=_params(2), name="filter_spec",
    )(hmlp, wout, wout, tcol, dabs, dfw)


def _hyena_kernel(h_ref, w_ref, cwv_ref, cw1_ref, cw2_ref,
                  cbv_ref, cb1_ref, cb2_ref, dfw_ref, dinv_ref,
                  gr0_ref, gi0_ref, gr1_ref, gi1_ref, fb0_ref, fb1_ref,
                  o_ref, rawa_ref, rawb_ref, u_ref, tw_ref, res_ref, y_ref, *, n):
    halo = CONV_HALO
    nb = FFT_BLOCK
    ct = CH_TILE
    zero_halo = jnp.zeros((halo, 2 * ct), F32)
    for raw_ref in (rawa_ref, rawb_ref):
        raw_ref[0:halo, :] = zero_halo
        raw_ref[halo + n:2 * halo + n, :] = zero_halo

    def project_chunk(raw_ref, pair, r0):
        raw_ref[pl.ds(halo + r0, CONV_ROWS), :] = jnp.dot(
            h_ref[0, pl.ds(r0, CONV_ROWS), :], w_ref[:, 2 * pair * ct:2 * (pair + 1) * ct],
            preferred_element_type=F32)

    def short_conv(raw_ref, half, r0, cw_ref, cb_ref):
        rows = CONV_ROWS + 2 * halo
        blk = raw_ref[pl.ds(r0, rows), half * ct:(half + 1) * ct]
        prev = pltpu.roll(blk, 1, axis=0)[halo:halo + CONV_ROWS]
        cur = blk[halo:halo + CONV_ROWS]
        nxt = pltpu.roll(blk, rows - 1, axis=0)[halo:halo + CONV_ROWS]
        return cb_ref[...] + prev * cw_ref[0:1, :] + cur * cw_ref[1:2, :] + nxt * cw_ref[2:3, :]

    def long_conv(gr_ref, gi_ref, fwd_side=None, mxu_side=None, inv_side=None):
        _fft_stage1(u_ref, tw_ref, fwd_side)
        for k in range(FFT_RES):
            x = _fft_forward_mxu(tw_ref, dfw_ref, k)
            xr, xi = x[0:nb], x[nb:2 * nb]
            gr, gi = gr_ref[0, k], gi_ref[0, k]
            y = jnp.concatenate([xr * gr - xi * gi, xr * gi + xi * gr], axis=0).astype(BF16)
            if k in (0, FFT_RES - 1):
                res_ref[k, 0:nb, :] = jnp.dot(dinv_ref[k, 0:nb, :], y, preferred_element_type=F32)
            else:
                res_ref[k] = jnp.dot(dinv_ref[k], y, preferred_element_type=F32)
            if mxu_side is not None:
                mxu_side(k)
        _fft_stage2_inverse(res_ref, y_ref, inv_side)

    def stash(half, r0, val):
        rawa_ref[pl.ds(halo + r0, CONV_ROWS), half * ct:(half + 1) * ct] = val

    def conv_out_times_stashed(half, r0, fb_ref):
        rows = pl.ds(r0, CONV_ROWS)
        mult = rawa_ref[pl.ds(halo + r0, CONV_ROWS), half * ct:(half + 1) * ct]
        return mult * (y_ref[rows, :] + u_ref[rows, :] * fb_ref[0])

    def store_u(r0, u):
        u_ref[pl.ds(r0, CONV_ROWS), :] = u

    n_chunks = n // CONV_ROWS
    for c in range(n_chunks + 1):
        if c < n_chunks:
            project_chunk(rawa_ref, 0, c * CONV_ROWS)
        if c > 0:
            store_u((c - 1) * CONV_ROWS, short_conv(rawa_ref, 0, (c - 1) * CONV_ROWS, cwv_ref, cbv_ref))

    pending = [functools.partial(project_chunk, rawb_ref, 1, c * CONV_ROWS) for c in range(n_chunks)]

    def drain(ticks):
        def side(i):
            if i in ticks and pending:
                pending.pop(0)()
        return side

    def x1_conv(k):
        if k < n_chunks:
            stash(0, k * CONV_ROWS, short_conv(rawa_ref, 1, k * CONV_ROWS, cw1_ref, cb1_ref))

    def x2_conv_gate(k):
        if k < n_chunks:
            gate = rawb_ref[pl.ds(halo + k * CONV_ROWS, CONV_ROWS), ct:2 * ct]
            stash(1, k * CONV_ROWS, short_conv(rawb_ref, 0, k * CONV_ROWS, cw2_ref, cb2_ref) * _silu(gate))

    long_conv(gr0_ref, gi0_ref, fwd_side=drain((3, 7)), mxu_side=x1_conv, inv_side=drain((1, 4, 7)))
    for c in range(n_chunks):
        store_u(c * CONV_ROWS, conv_out_times_stashed(0, c * CONV_ROWS, fb0_ref))
        drain((1, 4, 7))(c)
    assert not pending
    no_side = lambda i: None
    long_conv(gr1_ref, gi1_ref, fwd_side=no_side, mxu_side=x2_conv_gate, inv_side=no_side)

    def finish(r0, carry):
        o_ref[0, pl.ds(r0, CONV_ROWS), :] = conv_out_times_stashed(1, r0, fb1_ref).astype(BF16)
        return carry
    _row_chunks(n, finish)


def _hyena(h1, w_tiles_bf16, conv_w, conv_b, dfw, dinv, gr, gi, fbias):
    b, n, d = h1.shape
    e = gr.shape[3]
    ct = CH_TILE
    nct = e // ct
    nb = FFT_BLOCK
    cwspec = lambda k: pl.BlockSpec((SHORT_CONV, ct), lambda j, i, k=k: (0, k * nct + j))
    cbspec = lambda k: pl.BlockSpec((1, ct), lambda j, i, k=k: (0, k * nct + j))
    dspec = pl.BlockSpec((FFT_RES, 2 * nb, 2 * nb), lambda j, i: (0, 0, 0), pipeline_mode=pl.Buffered(1))
    gspec = lambda o: pl.BlockSpec((1, FFT_RES, nb, ct), lambda j, i, o=o: (o, 0, 0, j),
                                   pipeline_mode=pl.Buffered(1))
    vspec = lambda o: pl.BlockSpec((1, 1, ct), lambda j, i, o=o: (o, 0, j))
    kern = functools.partial(_hyena_kernel, n=n)
    return pl.pallas_call(
        kern,
        out_shape=jax.ShapeDtypeStruct((b, n, e), BF16),
        grid=(nct, b),
        in_specs=[pl.BlockSpec((1, n, d), lambda j, i: (i, 0, 0)),
                  pl.BlockSpec((d, 4 * ct), lambda j, i: (0, j)),
                  cwspec(0), cwspec(1), cwspec(2), cbspec(0), cbspec(1), cbspec(2),
                  dspec, dspec,
                  gspec(0), gspec(0), gspec(1), gspec(1), vspec(0), vspec(1)],
        out_specs=pl.BlockSpec((1, n, ct), lambda j, i: (i, 0, j)),
        scratch_shapes=[pltpu.VMEM((n + 2 * CONV_HALO, 2 * ct), F32), pltpu.VMEM((n + 2 * CONV_HALO, 2 * ct), F32),
                        pltpu.VMEM((n, ct), F32),
                        pltpu.VMEM((FFT_RES, 2 * nb, ct), BF16), pltpu.VMEM((FFT_RES, 2 * nb, ct), F32),
                        pltpu.VMEM((n, ct), F32)],
        compiler_params=_params(2), name="hyena_mix",
    )(h1, w_tiles_bf16, conv_w, conv_w, conv_w, conv_b, conv_b, conv_b,
      dfw, dinv, gr, gi, gr, gi, fbias, fbias)


def _out1_kernel(z_ref, x_ref, gate_ref, w_ref, fg_ref, o_ref):
    y = jnp.dot(z_ref[0], w_ref[...], preferred_element_type=F32)
    x2 = x_ref[0] + gate_ref[0] * y
    o_ref[0] = (x2 * lax.rsqrt(jnp.mean(x2 * x2, axis=-1, keepdims=True) + RMS_EPS)) * fg_ref[...]


def _out1(zg, x1, gate, w_bf16, final_g):
    b, n, d = x1.shape
    e = zg.shape[2]
    tl = ROW_TILE
    row = lambda i, j: (i, j, 0)
    return pl.pallas_call(
        _out1_kernel,
        out_shape=jax.ShapeDtypeStruct((b, n, d), F32),
        grid=(b, n // tl),
        in_specs=[pl.BlockSpec((1, tl, e), row), pl.BlockSpec((1, tl, d), row),
                  pl.BlockSpec((1, 1, d), lambda i, j: (i, 0, 0)),
                  pl.BlockSpec((e, d), lambda i, j: (0, 0)),
                  pl.BlockSpec((1, d), lambda i, j: (0, 0))],
        out_specs=pl.BlockSpec((1, tl, d), row),
        compiler_params=_params(2), name="out1",
    )(zg, x1, gate, w_bf16, final_g)


def _rope_tables(n):
    rows = n // GRID_W
    row = jnp.repeat(jnp.arange(rows), GRID_W).astype(F32)
    col = jnp.tile(jnp.arange(GRID_W), rows).astype(F32)
    half = A_HEAD_DIM // 2
    inv = ROPE_THETA ** (-jnp.arange(0, half, 2, dtype=F32) / half)
    ar, ac = row[:, None] * inv, col[:, None] * inv
    ang = jnp.concatenate([ar, ar, ac, ac], axis=-1)
    cos, sin = jnp.cos(ang), jnp.sin(ang)
    first = (jnp.arange(A_HEAD_DIM) % (half)) < (half // 2)
    sin_lo = jnp.where(first, -sin, 0.0)
    sin_hi = jnp.where(first, 0.0, sin)
    rep = V7X_LANES // A_HEAD_DIM
    return jnp.tile(cos, (1, rep)), jnp.tile(sin_lo, (1, rep)), jnp.tile(sin_hi, (1, rep))


def _fft_tables(n):
    nb = FFT_BLOCK
    idx = jnp.arange(nb, dtype=jnp.int32)
    k2 = jnp.arange(FFT_RES, dtype=jnp.int32)
    freq = FFT_RADIX * idx[None, :, None] + k2[:, None, None]
    ang = ((freq * idx[None, None, :]) % (2 * n)).astype(F32) * (math.pi / n)
    c, s = jnp.cos(ang), jnp.sin(ang)
    dfw = jnp.concatenate([jnp.concatenate([c, s], axis=2), jnp.concatenate([-s, c], axis=2)], axis=1)
    return dfw.astype(BF16), jnp.swapaxes(dfw, 1, 2).astype(BF16)


def _filter_features(n, kpad):
    t = jnp.linspace(0.0, 1.0, n, dtype=F32)[:, None]
    w = 2.0 * math.pi * jnp.arange(n, dtype=F32)[:, None] / n
    bands = jnp.linspace(1e-4, FILTER_BANDS - 1, FILTER_BANDS, dtype=F32)[None, :]
    z = jnp.concatenate([t, jnp.cos(bands * w), -jnp.sin(bands * w)], axis=-1)
    return jnp.pad(z, ((0, 0), (0, kpad - z.shape[1]))), t


def _pad2(a, rows, cols):
    return jnp.pad(a, ((0, rows - a.shape[0]), (0, cols - a.shape[1])))


def kernel(x, c, ctx, c_ctx, norm_g, ada_w, ada_b, final_g, a_w_in, a_lam_q1, a_lam_k1, a_lam_q2, a_lam_k2, a_subln_g, a_pool_w, a_pool_scale, a_w_out, h_w_in, h_conv_w, h_conv_b, h_filt_w0, h_filt_b0, h_filt_f0, h_filt_w1, h_filt_b1, h_filt_f1, h_filt_w2, h_filt_b2, h_filt_f2, h_filt_wout, h_filt_bias, h_w_out):
    b, n, d = x.shape
    nc = ctx.shape[1]
    e = h_w_out.shape[1]
    assert norm_g.shape[0] == 2 and a_w_in.shape[0] == 1 and h_w_in.shape[0] == 1
    assert b + 1 <= COND_ROWS and n % ROW_TILE == 0 and n % Q_TILE == 0 and (b * nc) % ROW_TILE == 0
    assert d == A_HEADS * A_V_DIM and e % CH_TILE == 0 and 2 * n == FFT_BLOCK * FFT_RADIX

    cond = jnp.concatenate([c, c_ctx[None, :], jnp.zeros((COND_ROWS - b - 1, d), F32)], axis=0)
    mods = _ada(cond, ada_w, ada_b)
    shift = mods[:, :, 0:d]
    scale = mods[:, :, d:2 * d]
    gate = mods[:, :, 2 * d:3 * d]
    per_batch = lambda m, i: m[i, :b].reshape(b, 1, d)

    lam_init = 0.8 - 0.6 * math.exp(-0.3 * 0)
    w0 = a_w_in[0].astype(BF16)
    cos, sin_lo, sin_hi = _rope_tables(n)
    wvt = a_w_in[0, :, 2 * d:3 * d].T.astype(BF16)
    q, k, vt, p, g = _proj0(x, norm_g[0:1], per_batch(shift, 0), per_batch(scale, 0), cos, sin_lo, sin_hi,
                            w0, wvt)
    kc, vct = _ctx_kv(ctx.reshape(b * nc, d), norm_g[0:1], shift[0, b:b + 1], scale[0, b:b + 1], w0, wvt)
    o = _attention(q, kc.reshape(b, nc, d), k, vct, vt,
                   a_lam_q1, a_lam_k1, a_lam_q2, a_lam_k2, a_subln_g, lam_init)
    ypool = _pool(p, a_pool_w[0].astype(BF16), a_pool_scale)
    x1, h1 = _out0(o, ypool, g, x, per_batch(gate, 0), a_w_out[0].astype(BF16),
                   norm_g[1:2], per_batch(shift, 1), per_batch(scale, 1))

    kpad = V7X_LANES
    z, tcol = _filter_features(n, kpad)
    row1 = lambda a: _pad2(a, 1, kpad)
    hmlp = _filter_mlp(z, _pad2(h_filt_w0[0], kpad, kpad), row1(h_filt_b0), row1(h_filt_f0),
                       _pad2(h_filt_w1[0], kpad, kpad), row1(h_filt_b1), row1(h_filt_f1),
                       _pad2(h_filt_w2[0], kpad, kpad), row1(h_filt_b2), row1(h_filt_f2))
    max_decay = math.log(DECAY_TARGET) / FAST_DECAY_PCT
    min_decay = math.log(DECAY_TARGET) / SLOW_DECAY_PCT
    dabs = jnp.abs(jnp.linspace(min_decay, max_decay, e, dtype=F32))[None, :]
    dfw, dinv = _fft_tables(n)
    wout = _pad2(h_filt_wout[0], kpad, h_filt_wout.shape[2])
    gr, gi = _filter_spec(hmlp, wout, tcol, dabs, dfw)
    nct = e // CH_TILE
    w_tiles = (h_w_in[0].astype(BF16).reshape(d, H_ORDER + 2, nct, CH_TILE)
               .transpose(0, 2, 1, 3).reshape(d, (H_ORDER + 2) * e))
    zg = _hyena(h1, w_tiles, h_conv_w[0], h_conv_b, dfw, dinv, gr, gi,
                h_filt_bias[0].reshape(H_ORDER, 1, e))
    return _out1(zg, x1, per_batch(gate, 1), h_w_out[0].astype(BF16), final_g[None, :])
```
